```python
import jax, jax.numpy as jnp
from jax import lax
import numpy as np

D_MODEL = 2048
BATCH = 4
SEQ = 2048
DEPTH = 4
DEC_BATCH = 128
DEC_SEQ = 1
PAST_LEN = 16384
PAGE_SIZE = 128

W_A = D_MODEL // 4
A_GROUPS = 4
A_GD = W_A // A_GROUPS
A_CHUNK = 128
W_B = 3 * D_MODEL // 8
B_HEAD = 64
B_HEADS = W_B // B_HEAD
LORA_W = 64
LORA_A = 64
LORA_G = 128
W_C = D_MODEL - W_A - W_B
C_HEAD = 128
C_HEADS = W_C // C_HEAD
C_CHUNK = 128
MIX_WIDTH = W_A + W_B + W_C
P_A = 2 * W_A
P_B = 3 * W_B + LORA_W + LORA_A + LORA_G
P_C = 4 * W_C
P_IN = P_A + P_B + P_C
D_FF = (8 * D_MODEL + 3 * 256 - 1) // (3 * 256) * 256
D_PLE = 256
DEEPNORM_ALPHA = (2 * DEPTH) ** 0.25
DEEPNORM_BETA = (8 * DEPTH) ** -0.25
LN_EPS = 1e-5
B_GN_EPS = 1e-5 * B_HEAD
RMS_EPS = 1e-6

kernel_name = 'hybrid_sgu_rwkv7_hgrn2_deepnorm_step'


def layer_norm(x, g, b, eps=LN_EPS):
    xf = x.astype(jnp.float32)
    mu = jnp.mean(xf, axis=-1, keepdims=True)
    var = jnp.mean(jnp.square(xf - mu), axis=-1, keepdims=True)
    return ((xf - mu) * lax.rsqrt(var + eps) * g.astype(jnp.float32) + b.astype(jnp.float32)).astype(x.dtype)


def rms_norm(x, g, eps=RMS_EPS):
    xf = x.astype(jnp.float32)
    return (xf * lax.rsqrt(jnp.mean(jnp.square(xf), axis=-1, keepdims=True) + eps) * g.astype(jnp.float32)).astype(x.dtype)


def chunk_sgu(u, v, ln_g, ln_b, w_s, b_s):
    B, T, _ = u.shape
    n_chunks = -(-T // A_CHUNK)
    pad = n_chunks * A_CHUNK - T
    vn = layer_norm(v.reshape(B, T, A_GROUPS, A_GD), ln_g.reshape(A_GROUPS, A_GD), ln_b.reshape(A_GROUPS, A_GD))
    vb = jnp.pad(vn, ((0, 0), (0, pad), (0, 0), (0, 0))).reshape(B, n_chunks, A_CHUNK, A_GROUPS, A_GD)
    causal = jnp.tril(jnp.ones((A_CHUNK, A_CHUNK), dtype=bool))
    w_causal = jnp.where(causal[None], w_s, jnp.zeros_like(w_s))
    mixed = jnp.einsum('gts,bcsgd->bctgd', w_causal, vb) + b_s.T[None, None, :, :, None]
    mixed = mixed.reshape(B, n_chunks * A_CHUNK, W_A)[:, :T]
    return u * mixed, vn.reshape(B, T, W_A)


def rwkv7_scan(r, log_w, k, v, kk, a, S0):
    xs = tuple(jnp.swapaxes(t.astype(jnp.float32), 0, 1) for t in (r, log_w, k, v, kk, a))

    def step(S, xt):
        r_t, lw_t, k_t, v_t, kk_t, a_t = xt
        s_kk = jnp.einsum('bhvk,bhk->bhv', S, kk_t)
        S = (S * jnp.exp(lw_t)[:, :, None, :]
             - s_kk[..., None] * (kk_t * a_t)[:, :, None, :]
             + v_t[..., None] * k_t[:, :, None, :])
        return S, jnp.einsum('bhvk,bhk->bhv', S, r_t)

    S, ys = lax.scan(step, S0.astype(jnp.float32), xs)
    return jnp.swapaxes(ys, 0, 1), S


def rwkv7_mix(zb, S0, W):
    B, T, _ = zb.shape
    o1, o2, o3 = W_B, 2 * W_B, 3 * W_B
    o4, o5 = o3 + LORA_W, o3 + LORA_W + LORA_A
    r, k, v = zb[..., :o1], zb[..., o1:o2], zb[..., o2:o3]
    wd, ad, gd = zb[..., o3:o4], zb[..., o4:o5], zb[..., o5:]
    w = (W['b_w0'] + jnp.tanh(wd) @ W['b_w_up']).astype(jnp.float32)
    log_decay = -jnp.exp(-jax.nn.softplus(-w) - 0.5)
    a = jax.nn.sigmoid(W['b_a0'] + ad @ W['b_a_up'])
    g = jax.nn.sigmoid(gd) @ W['b_g_up']
    heads = lambda t: t.reshape(B, T, B_HEADS, B_HEAD)
    kk = heads((k * W['b_k_k']).astype(jnp.float32))
    kk = kk / jnp.maximum(jnp.sqrt(jnp.sum(jnp.square(kk), axis=-1, keepdims=True)), 1e-12)
    k = k * (1 + (a - 1) * W['b_k_a'])
    rh, kh, vh, ah = heads(r), heads(k), heads(v), heads(a)
    y, S = rwkv7_scan(rh, heads(log_decay), kh, vh, kk, ah, S0)
    y = layer_norm(y, W['b_gn_g'].reshape(B_HEADS, B_HEAD), W['b_gn_b'].reshape(B_HEADS, B_HEAD), eps=B_GN_EPS)
    bonus = jnp.sum((rh * kh * W['b_r_k']).astype(jnp.float32), axis=-1, keepdims=True) * vh.astype(jnp.float32)
    y = (y + bonus).reshape(B, T, W_B) * g.astype(jnp.float32)
    return y.astype(zb.dtype), S


def hgrn2_chunkwise(q, log_f, k, v, S0):
    B, T, H, DK = q.shape
    DV = v.shape[-1]
    C = min(C_CHUNK, T)
    n = -(-T // C)
    pad = n * C - T

    def blocks(t):
        t = jnp.pad(t.astype(jnp.float32), ((0, 0), (0, pad), (0, 0), (0, 0)))
        return jnp.swapaxes(t.reshape(B, n, C, H, t.shape[-1]), 0, 1)

    causal = jnp.tril(jnp.ones((C, C), dtype=bool))[None, :, :, None, None]

    def step(S, blk):
        qc, lfc, kc, vc = blk
        b = jnp.cumsum(lfc, axis=1)
        o_inter = jnp.einsum('bthk,bhkv->bthv', qc * jnp.exp(b), S)
        diff = b[:, :, None] - b[:, None, :]
        dec = jnp.where(causal, jnp.exp(jnp.minimum(diff, 0.0)), 0.0)
        att = jnp.einsum('bthk,bshk,btshk->bhts', qc, kc, dec)
        o_intra = jnp.einsum('bhts,bshv->bthv', att, vc)
        b_end = b[:, -1]
        S = jnp.exp(b_end)[..., None] * S + jnp.einsum('bshk,bshv->bhkv', kc * jnp.exp(b_end[:, None] - b), vc)
        return S, o_inter + o_intra

    S, o = lax.scan(step, S0.astype(jnp.float32), tuple(blocks(t) for t in (q, log_f, k, v)))
    o = jnp.swapaxes(o, 0, 1).reshape(B, n * C, H, DV)[:, :T]
    return o, S


def hgrn2_mix(zc, lb, S0, W):
    B, T, _ = zc.shape
    q, fz, iv, g = jnp.split(zc, 4, axis=-1)
    fz = fz.astype(jnp.float32)
    log_f = jnp.logaddexp(jax.nn.log_sigmoid(fz), jnp.log(lb) + jax.nn.log_sigmoid(-fz))
    k = (1.0 - lb) * jax.nn.sigmoid(-fz)
    heads = lambda t: t.reshape(B, T, C_HEADS, C_HEAD)
    o, S = hgrn2_chunkwise(heads(q), heads(log_f), heads(k), heads(iv), S0)
    o = rms_norm(o.reshape(B, T, W_C), W['c_norm_g']) * jax.nn.silu(g.astype(jnp.float32))
    return o.astype(zc.dtype), S


def hybrid_layer(h, p_l, shift0, rwkv0, hgrn0, lb, W):
    z = h @ W['w_in']
    za, zb, zc = z[..., :P_A], z[..., P_A:P_A + P_B], z[..., P_A + P_B:]
    u = jax.nn.gelu(za[..., :W_A], approximate=False)
    v = jax.nn.gelu(za[..., W_A:], approximate=False)
    y_a, v_rows = chunk_sgu(u, v, W['a_ln_g'], W['a_ln_b'], W['a_ws'], W['a_bs'])
    prev = jnp.concatenate([shift0[:, None].astype(zb.dtype), zb[:, :-1]], axis=1)
    y_b, rwkv_new = rwkv7_mix(zb + W['b_mu'] * (prev - zb), rwkv0, W)
    y_c, hgrn_new = hgrn2_mix(zc, lb, hgrn0, W)
    mix = jnp.concatenate([y_a, y_b, y_c], axis=-1) @ W['w_out']
    h = layer_norm(DEEPNORM_ALPHA * h + mix, W['ln1_g'], W['ln1_b'])
    ffn = (jax.nn.silu(h @ W['w_ffn_gate']) * (h @ W['w_ffn_up'])) @ W['w_ffn_down']
    ple = jax.nn.sigmoid(h @ W['w_ple_gate']) * (p_l.astype(h.dtype) @ W['w_ple_proj'])
    h = layer_norm(DEEPNORM_ALPHA * h + ffn + ple, W['ln2_g'], W['ln2_b'])
    return h, zb[:, -1], rwkv_new.astype(h.dtype), hgrn_new.astype(h.dtype), v_rows


def setup_inputs(seed: int = 0) -> dict:
    key = jax.random.key(seed)
    ks = jax.random.split(key, 40)
    f32 = jnp.float32

    def nrm(i, shape, scale=1.0):
        return scale * jax.random.normal(ks[i], shape, f32)

    return {
        'x_prompt': nrm(0, (BATCH, SEQ, D_MODEL)),
        'x_sample': nrm(1, (DEC_BATCH, DEC_SEQ, D_MODEL)),
        'state_rwkv': nrm(2, (DEPTH, DEC_BATCH, B_HEADS, B_HEAD, B_HEAD), 0.3),
        'state_shift': nrm(3, (DEPTH, DEC_BATCH, P_B)),
        'state_hgrn': nrm(4, (DEPTH, DEC_BATCH, C_HEADS, C_HEAD, C_HEAD), 0.3),
        'p_prompt': nrm(5, (DEPTH, BATCH, SEQ, D_PLE)),
        'p_sample': nrm(6, (DEPTH, DEC_BATCH, DEC_SEQ, D_PLE)),
        'ln_in_g': 1.0 + nrm(7, (D_MODEL,), 0.02),
        'ln_in_b': nrm(8, (D_MODEL,), 0.02),
        'w_in': nrm(9, (DEPTH, D_MODEL, P_IN), D_MODEL ** -0.5),
        'a_ln_g': 1.0 + nrm(10, (DEPTH, W_A), 0.02),
        'a_ln_b': nrm(11, (DEPTH, W_A), 0.02),
        'a_ws': nrm(12, (DEPTH, A_GROUPS, A_CHUNK, A_CHUNK), A_CHUNK ** -0.5),
        'a_bs': 1.0 + nrm(13, (DEPTH, A_GROUPS, A_CHUNK), 0.02),
        'b_mu': jax.random.uniform(ks[14], (DEPTH, P_B), f32, 0.1, 0.9),
        'b_w0': jax.random.uniform(ks[15], (DEPTH, W_B), f32, -6.0, 1.0),
        'b_w_up': nrm(16, (DEPTH, LORA_W, W_B), 0.5 * LORA_W ** -0.5),
        'b_a0': nrm(17, (DEPTH, W_B), 0.1),
        'b_a_up': nrm(18, (DEPTH, LORA_A, W_B), 0.5 * LORA_A ** -0.5),
        'b_g_up': nrm(19, (DEPTH, LORA_G, W_B), LORA_G ** -0.5),
        'b_k_k': 0.85 + nrm(20, (DEPTH, W_B), 0.05),
        'b_k_a': 1.0 + nrm(21, (DEPTH, W_B), 0.05),
        'b_r_k': nrm(22, (DEPTH, B_HEADS, B_HEAD), 0.1),
        'b_gn_g': 1.0 + nrm(23, (DEPTH, W_B), 0.02),
        'b_gn_b': nrm(24, (DEPTH, W_B), 0.02),
        'c_lower_bounds': nrm(25, (DEPTH, W_C)),
        'c_norm_g': 1.0 + nrm(26, (DEPTH, W_C), 0.02),
        'w_out': nrm(27, (DEPTH, MIX_WIDTH, D_MODEL), DEEPNORM_BETA * MIX_WIDTH ** -0.5),
        'ln1_g': 1.0 + nrm(28, (DEPTH, D_MODEL), 0.02),
        'ln1_b': nrm(29, (DEPTH, D_MODEL), 0.02),
        'w_ffn_gate': nrm(30, (DEPTH, D_MODEL, D_FF), D_MODEL ** -0.5),
        'w_ffn_up': nrm(31, (DEPTH, D_MODEL, D_FF), D_MODEL ** -0.5),
        'w_ffn_down': nrm(32, (DEPTH, D_FF, D_MODEL), DEEPNORM_BETA * D_FF ** -0.5),
        'w_ple_gate': nrm(33, (DEPTH, D_MODEL, D_MODEL), D_MODEL ** -0.5),
        'w_ple_proj': nrm(34, (DEPTH, D_PLE, D_MODEL), DEEPNORM_BETA * D_PLE ** -0.5),
        'ln2_g': 1.0 + nrm(35, (DEPTH, D_MODEL), 0.02),
        'ln2_b': nrm(36, (DEPTH, D_MODEL), 0.02),
    }


def reference(x_prompt, x_sample, state_rwkv, state_shift, state_hgrn, p_prompt, p_sample,
              ln_in_g, ln_in_b, w_in, a_ln_g, a_ln_b, a_ws, a_bs,
              b_mu, b_w0, b_w_up, b_a0, b_a_up, b_g_up, b_k_k, b_k_a, b_r_k, b_gn_g, b_gn_b,
              c_lower_bounds, c_norm_g, w_out, ln1_g, ln1_b,
              w_ffn_gate, w_ffn_up, w_ffn_down, w_ple_gate, w_ple_proj, ln2_g, ln2_b):
    lb_cum = jnp.cumsum(jax.nn.softmax(c_lower_bounds.astype(jnp.float32), axis=0), axis=0)
    lower_bound = lb_cum - lb_cum[0]
    n_prompt = x_prompt.shape[0]
    dt = x_prompt.dtype
    hp = layer_norm(x_prompt, ln_in_g, ln_in_b)
    hs = layer_norm(x_sample, ln_in_g, ln_in_b)
    shift_zero = jnp.zeros((n_prompt, P_B), dt)
    rwkv_zero = jnp.zeros((n_prompt, B_HEADS, B_HEAD, B_HEAD), dt)
    hgrn_zero = jnp.zeros((n_prompt, C_HEADS, C_HEAD, C_HEAD), dt)
    rwkv_p, shift_p, hgrn_p = [], [], []
    rwkv_s, shift_s, hgrn_s, sgu_v_s = [], [], [], []
    for i in range(DEPTH):
        W = dict(w_in=w_in[i], a_ln_g=a_ln_g[i], a_ln_b=a_ln_b[i], a_ws=a_ws[i], a_bs=a_bs[i],
                 b_mu=b_mu[i], b_w0=b_w0[i], b_w_up=b_w_up[i], b_a0=b_a0[i], b_a_up=b_a_up[i],
                 b_g_up=b_g_up[i], b_k_k=b_k_k[i], b_k_a=b_k_a[i], b_r_k=b_r_k[i],
                 b_gn_g=b_gn_g[i], b_gn_b=b_gn_b[i], c_norm_g=c_norm_g[i], w_out=w_out[i],
                 ln1_g=ln1_g[i], ln1_b=ln1_b[i], w_ffn_gate=w_ffn_gate[i], w_ffn_up=w_ffn_up[i],
                 w_ffn_down=w_ffn_down[i], w_ple_gate=w_ple_gate[i], w_ple_proj=w_ple_proj[i],
                 ln2_g=ln2_g[i], ln2_b=ln2_b[i])
        hp, sh, r_st, c_st, _ = hybrid_layer(hp, p_prompt[i], shift_zero, rwkv_zero, hgrn_zero, lower_bound[i], W)
        rwkv_p.append(r_st)
        shift_p.append(sh)
        hgrn_p.append(c_st)
        hs, sh, r_st, c_st, v_rows = hybrid_layer(hs, p_sample[i], state_shift[i], state_rwkv[i], state_hgrn[i], lower_bound[i], W)
        rwkv_s.append(r_st)
        shift_s.append(sh)
        hgrn_s.append(c_st)
        sgu_v_s.append(v_rows)
    return (hp, hs, jnp.stack(rwkv_p), jnp.stack(shift_p), jnp.stack(hgrn_p),
            jnp.stack(rwkv_s), jnp.stack(shift_s), jnp.stack(hgrn_s), jnp.stack(sgu_v_s))
```

```python
import functools

import jax
import jax.numpy as jnp
from jax import lax
from jax.experimental import pallas as pl
from jax.experimental.pallas import tpu as pltpu

F32 = jnp.float32
BF16 = jnp.bfloat16
HIGHEST = lax.Precision.HIGHEST

A_GROUPS = 4
A_CHUNK = 128
B_HEAD = 64
LORA_W, LORA_A, LORA_G = 64, 64, 128
C_HEAD = 128
C_CHUNK = 128
LN_EPS = 1e-5
B_GN_EPS = 1e-5 * B_HEAD
RMS_EPS = 1e-6

RWKV_CHUNK = 64
HGRN_SUB = 16
SAMPLE_ROWS = 16
ROW_BLOCK = 128
VMEM_LIMIT_BYTES = 56 * 1024 * 1024


def _cparams(*sem):
    return pltpu.CompilerParams(dimension_semantics=sem, vmem_limit_bytes=VMEM_LIMIT_BYTES)


def _pick_tile(n, target, align):
    best = None
    for t in range(align, min(n, target) + 1, align):
        if n % t == 0:
            best = t
    assert best is not None, (n, target, align)
    return best


def _dot(a, b, prec=None):
    return jnp.dot(a, b, precision=prec, preferred_element_type=F32)


def _dot_nt(a, b, prec=None):
    return lax.dot_general(a, b, (((1,), (1,)), ((), ())), precision=prec, preferred_element_type=F32)


def _dot_tn(a, b, prec=None):
    return lax.dot_general(a, b, (((0,), (0,)), ((), ())), precision=prec, preferred_element_type=F32)


def _bdot(a, b):
    return jnp.dot(a.astype(BF16), b.astype(BF16), preferred_element_type=F32)


def _bdot_nt(a, b):
    return _dot_nt(a.astype(BF16), b.astype(BF16))


def _layer_norm(x, g, b, eps):
    mu = jnp.mean(x, axis=-1, keepdims=True)
    xc = x - mu
    var = jnp.mean(xc * xc, axis=-1, keepdims=True)
    return xc * lax.rsqrt(var + eps) * g + b


def _gelu(x):
    return 0.5 * x * (1.0 + lax.erf(x * 0.7071067811865476))


def _sigmoid(x):
    return 1.0 / (1.0 + jnp.exp(-x))


def _log_sigmoid(x):
    return jnp.minimum(x, 0.0) - jnp.log1p(jnp.exp(-jnp.abs(x)))


def _seg_sum(x, ones_bd):
    hi = x.astype(BF16)
    lo = (x - hi.astype(F32)).astype(BF16)
    return _dot(hi, ones_bd) + _dot(lo, ones_bd)


def _ln_in_kernel(xp_ref, xs_ref, g_ref, b_ref, h_ref, hb_ref, *, n_prompt_blocks):
    i = pl.program_id(0)

    def emit(x):
        h = _layer_norm(x, g_ref[...], b_ref[...], LN_EPS)
        h_ref[...] = h
        hb_ref[...] = h.astype(BF16)

    @pl.when(i < n_prompt_blocks)
    def _():
        emit(xp_ref[...])

    @pl.when(i >= n_prompt_blocks)
    def _():
        emit(xs_ref[...])


def _ln_in(xp, xs, g, b):
    n_p, d = xp.shape
    n_s = xs.shape[0]
    npb, nsb = n_p // ROW_BLOCK, n_s // ROW_BLOCK
    m = n_p + n_s
    return pl.pallas_call(
        functools.partial(_ln_in_kernel, n_prompt_blocks=npb),
        grid=(npb + nsb,),
        in_specs=[
            pl.BlockSpec((ROW_BLOCK, d), lambda i: (jnp.minimum(i, npb - 1), 0)),
            pl.BlockSpec((ROW_BLOCK, d), lambda i: (jnp.maximum(i - npb, 0), 0)),
            pl.BlockSpec((1, d), lambda i: (0, 0)),
            pl.BlockSpec((1, d), lambda i: (0, 0)),
        ],
        out_specs=[pl.BlockSpec((ROW_BLOCK, d), lambda i: (i, 0)), pl.BlockSpec((ROW_BLOCK, d), lambda i: (i, 0))],
        out_shape=[jax.ShapeDtypeStruct((m, d), F32), jax.ShapeDtypeStruct((m, d), BF16)],
        compiler_params=_cparams("parallel"),
        name="ln_in",
    )(xp, xs, g, b)


def _mm_kernel(x_ref, w_ref, o_ref):
    o_ref[...] = jnp.dot(x_ref[...], w_ref[...], preferred_element_type=F32)


def _mm(xb, w, name):
    m, k = xb.shape
    n = w.shape[1]
    tm = _pick_tile(m, 640, 16)
    tn = _pick_tile(n, 512, 128)
    return pl.pallas_call(
        _mm_kernel,
        grid=(m // tm, n // tn),
        in_specs=[pl.BlockSpec((tm, k), lambda i, j: (i, 0)), pl.BlockSpec((k, tn), lambda i, j: (0, j))],
        out_specs=pl.BlockSpec((tm, tn), lambda i, j: (i, j)),
        out_shape=jax.ShapeDtypeStruct((m, n), F32),
        compiler_params=_cparams("parallel", "parallel"),
        name=name,
    )(xb, w)


def _sgu_kernel(u_ref, v_ref, lng_ref, lnb_ref, ws_ref, bst_ref, y_ref, vn_ref, *, n_prompt_blocks):
    i = pl.program_id(0)
    gd = u_ref.shape[1] // A_GROUPS
    u = _gelu(u_ref[...])
    v = _gelu(v_ref[...])
    vn = [
        _layer_norm(v[:, g * gd:(g + 1) * gd], lng_ref[:, g * gd:(g + 1) * gd], lnb_ref[:, g * gd:(g + 1) * gd], LN_EPS)
        for g in range(A_GROUPS)
    ]

    @pl.when(i < n_prompt_blocks)
    def _():
        row = lax.broadcasted_iota(jnp.int32, (A_CHUNK, A_CHUNK), 0)
        col = lax.broadcasted_iota(jnp.int32, (A_CHUNK, A_CHUNK), 1)
        for g in range(A_GROUPS):
            w_causal = jnp.where(row >= col, ws_ref[g], 0.0)
            mixed = _bdot(w_causal, vn[g]) + bst_ref[:, g:g + 1]
            y_ref[:, g * gd:(g + 1) * gd] = (u[:, g * gd:(g + 1) * gd] * mixed).astype(BF16)

    @pl.when(i >= n_prompt_blocks)
    def _():
        for g in range(A_GROUPS):
            mixed = vn[g] * ws_ref[g, 0:1, 0:1] + bst_ref[0:1, g:g + 1]
            y_ref[:, g * gd:(g + 1) * gd] = (u[:, g * gd:(g + 1) * gd] * mixed).astype(BF16)
            vn_ref[:, g * gd:(g + 1) * gd] = vn[g]


def _sgu(za, lng, lnb, ws, bst, n_prompt):
    m = za.shape[0]
    wa = za.shape[1] // 2
    npb = n_prompt // A_CHUNK
    nb = m // A_CHUNK
    n_s = m - n_prompt
    return pl.pallas_call(
        functools.partial(_sgu_kernel, n_prompt_blocks=npb),
        grid=(nb,),
        in_specs=[
            pl.BlockSpec((A_CHUNK, wa), lambda i: (i, 0)),
            pl.BlockSpec((A_CHUNK, wa), lambda i: (i, 1)),
            pl.BlockSpec((1, wa), lambda i: (0, 0)),
            pl.BlockSpec((1, wa), lambda i: (0, 0)),
            pl.BlockSpec((A_GROUPS, A_CHUNK, A_CHUNK), lambda i: (0, 0, 0)),
            pl.BlockSpec((A_CHUNK, A_GROUPS), lambda i: (0, 0)),
        ],
        out_specs=[
            pl.BlockSpec((A_CHUNK, wa), lambda i: (i, 0)),
            pl.BlockSpec((A_CHUNK, wa), lambda i: (jnp.maximum(i - npb, 0), 0)),
        ],
        out_shape=[jax.ShapeDtypeStruct((m, wa), BF16), jax.ShapeDtypeStruct((n_s, wa), F32)],
        compiler_params=_cparams("arbitrary"),
        name="sgu",
    )(za, za, lng, lnb, ws, bst)


def _rwkv_prep(zb, prev, mu, w0, wup, a0, aup, gup, k_k, k_a, r_k, ones_bd):
    wb = k_k.shape[1]
    xs = zb + mu * (prev - zb)
    r, k, v = xs[:, :wb], xs[:, wb:2 * wb], xs[:, 2 * wb:3 * wb]
    o4, o5 = 3 * wb + LORA_W, 3 * wb + LORA_W + LORA_A
    wd, ad, gd = xs[:, 3 * wb:o4], xs[:, o4:o5], xs[:, o5:]
    w = w0 + _bdot(jnp.tanh(wd), wup)
    softplus_neg_w = jnp.maximum(-w, 0.0) + jnp.log1p(jnp.exp(-jnp.abs(w)))
    log_decay = -jnp.exp(-softplus_neg_w - 0.5)
    a = _sigmoid(a0 + _bdot(ad, aup))
    g = _bdot(_sigmoid(gd), gup)
    kk = k * k_k
    kk = kk / jnp.maximum(jnp.sqrt(_seg_sum(kk * kk, ones_bd)), 1e-12)
    kd = k * (1.0 + (a - 1.0) * k_a)
    bonus = _seg_sum(r * kd * r_k, ones_bd) * v
    return r, log_decay, kd, v, kk, a, g, bonus


def _rwkv_finish(y, bonus, g, gn_g, gn_b, ones_bd):
    mean = _seg_sum(y, ones_bd) * (1.0 / B_HEAD)
    yc = y - mean
    var = _seg_sum(yc * yc, ones_bd) * (1.0 / B_HEAD)
    return (yc * lax.rsqrt(var + B_GN_EPS) * gn_g + gn_b + bonus) * g


def _rwkv_prompt_kernel(zb_ref, mu_ref, w0_ref, wup_ref, a0_ref, aup_ref, gup_ref, kk_ref, ka_ref, rk_ref,
                        gng_ref, gnb_ref, ones_ref, y_ref, sout_ref, state_scr, prev_scr):
    c_idx = pl.program_id(1)

    @pl.when(c_idx == 0)
    def _():
        state_scr[...] = jnp.zeros_like(state_scr)
        prev_scr[...] = jnp.zeros_like(prev_scr)

    zb = zb_ref[...]
    C = zb.shape[0]
    n_heads = state_scr.shape[0]
    N = B_HEAD
    ones_bd = ones_ref[...]
    row1 = lax.broadcasted_iota(jnp.int32, (C, 1), 0)
    prev = jnp.where(row1 == 0, prev_scr[...], pltpu.roll(zb, 1, 0))
    prev_scr[...] = zb[C - 1:C, :]
    r, lw, kd, v, kk, a, g, bonus = _rwkv_prep(
        zb, prev, mu_ref[...], w0_ref[...], wup_ref[...], a0_ref[...], aup_ref[...], gup_ref[...],
        kk_ref[...], ka_ref[...], rk_ref[...], ones_bd)

    row = lax.broadcasted_iota(jnp.int32, (C, C), 0)
    col = lax.broadcasted_iota(jnp.int32, (C, C), 1)
    incl = row >= col
    strict = row > col
    cum = _dot(jnp.where(incl, 1.0, 0.0), lw, HIGHEST)
    m = cum[C // 2 - 1:C // 2, :]
    cend = cum[C - 1:C, :]
    e_pos = jnp.exp(cum - m)
    e_neg = jnp.exp(m - cum)
    e_prev = jnp.exp(cum - lw - m)
    e_m = jnp.exp(m)
    e_end = jnp.exp(cend - m)
    rp = r * e_pos
    kkp = kk * e_prev
    kp = kd * e_neg
    bp = kk * a * e_neg
    rhat = rp * e_m
    kkhat = kkp * e_m
    ktil = kp * e_end
    btil = bp * e_end
    e_cend = e_end * e_m

    ys = []
    for h in range(n_heads):
        sl = slice(h * N, (h + 1) * N)
        s0 = state_scr[h]
        scores = _dot_nt(jnp.concatenate([rp[:, sl], kkp[:, sl]], axis=0),
                         jnp.concatenate([kp[:, sl], bp[:, sl]], axis=0), HIGHEST)
        a_rk = jnp.where(incl, scores[:C, :C], 0.0)
        a_rb = jnp.where(incl, scores[:C, C:], 0.0)
        a_kk = jnp.where(strict, scores[C:, :C], 0.0)
        nmat = jnp.where(strict, scores[C:, C:], 0.0)
        vh = v[:, sl]
        x = jnp.concatenate([kkhat[:, sl], _dot(a_kk, vh, HIGHEST)], axis=1)
        x = x - _dot(nmat, x, HIGHEST)
        pw = nmat
        span = 2
        while span < C:
            pw = _dot(pw, pw, HIGHEST)
            x = x + _dot(pw, x, HIGHEST)
            span *= 2
        w1, w2 = x[:, :N], x[:, N:]
        qs = _dot_nt(jnp.concatenate([w1, rhat[:, sl]], axis=0), s0, HIGHEST)
        u = -(qs[:C] + w2)
        vu = jnp.concatenate([vh, u], axis=0)
        ys.append(qs[C:] + _dot(jnp.concatenate([a_rk, a_rb], axis=1), vu, HIGHEST))
        state_scr[h] = s0 * e_cend[:, sl] + _dot_tn(
            vu, jnp.concatenate([ktil[:, sl], btil[:, sl]], axis=0), HIGHEST)

    y = jnp.concatenate(ys, axis=1)
    y_ref[...] = _rwkv_finish(y, bonus, g, gng_ref[...], gnb_ref[...], ones_bd).astype(BF16)

    @pl.when(c_idx == pl.num_programs(1) - 1)
    def _():
        sout_ref[0] = state_scr[...]


def _full(shape):
    nd = len(shape)
    return pl.BlockSpec(shape, lambda *_: (0,) * nd)


def _rwkv_param_specs(params):
    return [_full(p.shape) for p in params]


def _rwkv_prompt(zb, params, batch, seq):
    pb = zb.shape[1]
    wb = params[-1].shape[0]
    n_heads = wb // B_HEAD
    C = RWKV_CHUNK
    nc = seq // C
    return pl.pallas_call(
        _rwkv_prompt_kernel,
        grid=(batch, nc),
        in_specs=[pl.BlockSpec((C, pb), lambda b, c: (b * nc + c, 0))] + _rwkv_param_specs(params),
        out_specs=[
            pl.BlockSpec((C, wb), lambda b, c: (b * nc + c, 0)),
            pl.BlockSpec((1, n_heads, B_HEAD, B_HEAD), lambda b, c: (b, 0, 0, 0)),
        ],
        out_shape=[
            jax.ShapeDtypeStruct((batch * seq, wb), BF16),
            jax.ShapeDtypeStruct((batch, n_heads, B_HEAD, B_HEAD), F32),
        ],
        scratch_shapes=[pltpu.VMEM((n_heads, B_HEAD, B_HEAD), F32), pltpu.VMEM((1, pb), F32)],
        compiler_params=_cparams("arbitrary", "arbitrary"),
        name="rwkv_prompt",
    )(zb, *params)


def _rwkv_sample_kernel(zb_ref, shift_ref, mu_ref, w0_ref, wup_ref, a0_ref, aup_ref, gup_ref, kk_ref, ka_ref,
                        rk_ref, gng_ref, gnb_ref, ones_ref, s_ref, y_ref, sout_ref,
                        kk_scr, b_scr, w_scr, kd_scr, v_scr, r_scr, y_scr):
    n_rows, n_heads = s_ref.shape[0], s_ref.shape[1]
    N = B_HEAD
    ones_bd = ones_ref[...]
    r, lw, kd, v, kk, a, g, bonus = _rwkv_prep(
        zb_ref[...], shift_ref[...], mu_ref[...], w0_ref[...], wup_ref[...], a0_ref[...], aup_ref[...],
        gup_ref[...], kk_ref[...], ka_ref[...], rk_ref[...], ones_bd)
    kk_scr[...] = kk
    b_scr[...] = kk * a
    w_scr[...] = jnp.exp(lw)
    kd_scr[...] = kd
    v_scr[...] = v
    r_scr[...] = r
    pad = jnp.zeros((6, N), F32)

    def body(i, carry):
        kk_r, b_r, w_r = kk_scr[pl.ds(i, 1), :], b_scr[pl.ds(i, 1), :], w_scr[pl.ds(i, 1), :]
        kd_r, v_r, r_r = kd_scr[pl.ds(i, 1), :], v_scr[pl.ds(i, 1), :], r_scr[pl.ds(i, 1), :]
        ys = []
        for h in range(n_heads):
            sl = slice(h * N, (h + 1) * N)
            s0 = s_ref[i, h]
            s_kk = _dot_nt(kk_r[:, sl], s0, HIGHEST)
            left = jnp.concatenate([-s_kk, v_r[:, sl], pad], axis=0)
            right = jnp.concatenate([b_r[:, sl], kd_r[:, sl], pad], axis=0)
            s1 = s0 * w_r[:, sl] + _dot_tn(left, right, HIGHEST)
            sout_ref[i, h] = s1
            ys.append(_dot_nt(r_r[:, sl], s1, HIGHEST))
        y_scr[pl.ds(i, 1), :] = jnp.concatenate(ys, axis=1)
        return carry

    lax.fori_loop(0, n_rows, body, 0)
    y_ref[...] = _rwkv_finish(y_scr[...], bonus, g, gng_ref[...], gnb_ref[...], ones_bd).astype(BF16)


def _rwkv_sample(zb, shift, state, params, n_prompt):
    pb = zb.shape[1]
    wb = params[-1].shape[0]
    n_s, n_heads = state.shape[0], state.shape[1]
    R = SAMPLE_ROWS
    off = n_prompt // R
    sspec = pl.BlockSpec((R, n_heads, B_HEAD, B_HEAD), lambda i: (i, 0, 0, 0))
    return pl.pallas_call(
        _rwkv_sample_kernel,
        grid=(n_s // R,),
        in_specs=[pl.BlockSpec((R, pb), lambda i: (off + i, 0)), pl.BlockSpec((R, pb), lambda i: (i, 0))]
        + _rwkv_param_specs(params) + [sspec],
        out_specs=[pl.BlockSpec((R, wb), lambda i: (i, 0)), sspec],
        out_shape=[jax.ShapeDtypeStruct((n_s, wb), BF16), jax.ShapeDtypeStruct(state.shape, F32)],
        scratch_shapes=[pltpu.VMEM((R, wb), F32)] * 7,
        compiler_params=_cparams("parallel"),
        name="rwkv_sample",
    )(zb, shift, *params, state)


def _hgrn_gates(fz, clb, layer):
    ls_pos = _log_sigmoid(fz)
    if layer == 0:
        return ls_pos, _sigmoid(-fz)
    e = jnp.exp(clb - jnp.max(clb, axis=0, keepdims=True))
    sm = e / jnp.sum(e, axis=0, keepdims=True)
    lb = jnp.sum(sm[1:layer + 1], axis=0, keepdims=True)
    x1 = ls_pos
    x2 = jnp.log(lb) + _log_sigmoid(-fz)
    log_f = jnp.maximum(x1, x2) + jnp.log1p(jnp.exp(-jnp.abs(x1 - x2)))
    return log_f, (1.0 - lb) * _sigmoid(-fz)


def _hgrn_finish(o, g, norm_g):
    on = o * lax.rsqrt(jnp.mean(o * o, axis=-1, keepdims=True) + RMS_EPS) * norm_g
    return on * (g * _sigmoid(g))


def _hgrn_prompt_kernel(zc_ref, clb_ref, ng_ref, y_ref, sout_ref, state_scr, *, layer):
    c_idx = pl.program_id(1)

    @pl.when(c_idx == 0)
    def _():
        state_scr[...] = jnp.zeros_like(state_scr)

    z = zc_ref[...]
    C = z.shape[0]
    wc = z.shape[1] // 4
    n_heads = wc // C_HEAD
    D = C_HEAD
    SUB = HGRN_SUB
    nsub = C // SUB
    q, fz, iv, g = z[:, :wc], z[:, wc:2 * wc], z[:, 2 * wc:3 * wc], z[:, 3 * wc:]
    log_f, kg = _hgrn_gates(fz, clb_ref[...], layer)
    row = lax.broadcasted_iota(jnp.int32, (C, C), 0)
    col = lax.broadcasted_iota(jnp.int32, (C, C), 1)
    bcum = _dot(jnp.where(row >= col, 1.0, 0.0), log_f, HIGHEST)
    e_b = jnp.exp(bcum)
    b_end = bcum[C - 1:C, :]
    e_end = jnp.exp(b_end)
    k_hat = kg * jnp.exp(b_end - bcum)
    t_in_sub = lax.broadcasted_iota(jnp.int32, (nsub, SUB, 1), 1)

    outs = []
    for h in range(n_heads):
        sl = slice(h * D, (h + 1) * D)
        qh, kh, vh, bh = q[:, sl], kg[:, sl], iv[:, sl], bcum[:, sl]
        st = state_scr[h]
        o = _bdot_nt(qh * e_b[:, sl], st)
        q3, k3, v3, b3 = (t.reshape(nsub, SUB, D) for t in (qh, kh, vh, bh))
        od = jnp.zeros((nsub, SUB, D), F32)
        for j in range(SUB):
            dec = jnp.exp(jnp.minimum(b3 - b3[:, j:j + 1, :], 0.0))
            att = jnp.sum(q3 * k3[:, j:j + 1, :] * dec, axis=-1, keepdims=True)
            att = jnp.where(t_in_sub >= j, att, 0.0)
            od = od + att * v3[:, j:j + 1, :]
        o = o + od.reshape(C, D)
        off_rows = [jnp.zeros((SUB, D), F32)]
        for i in range(1, nsub):
            lo = i * SUB
            b_bound = bh[lo - 1:lo, :]
            qi = qh[lo:lo + SUB, :] * jnp.exp(bh[lo:lo + SUB, :] - b_bound)
            ki = kh[:lo, :] * jnp.exp(b_bound - bh[:lo, :])
            att = _bdot_nt(qi, ki)
            off_rows.append(_bdot(att, vh[:lo, :]))
        outs.append(o + jnp.concatenate(off_rows, axis=0))
        state_scr[h] = st * e_end[:, sl] + _dot_tn(vh, k_hat[:, sl], HIGHEST)

    o_all = jnp.concatenate(outs, axis=1)
    y_ref[...] = _hgrn_finish(o_all, g, ng_ref[...]).astype(BF16)

    @pl.when(c_idx == pl.num_programs(1) - 1)
    def _():
        for h in range(n_heads):
            sout_ref[0, h] = state_scr[h].T


def _hgrn_prompt(zc, clb, norm_g, batch, seq, layer):
    pc = zc.shape[1]
    wc = pc // 4
    n_heads = wc // C_HEAD
    C = C_CHUNK
    nc = seq // C
    return pl.pallas_call(
        functools.partial(_hgrn_prompt_kernel, layer=layer),
        grid=(batch, nc),
        in_specs=[pl.BlockSpec((C, pc), lambda b, c: (b * nc + c, 0)), _full(clb.shape), _full(norm_g.shape)],
        out_specs=[
            pl.BlockSpec((C, wc), lambda b, c: (b * nc + c, 0)),
            pl.BlockSpec((1, n_heads, C_HEAD, C_HEAD), lambda b, c: (b, 0, 0, 0)),
        ],
        out_shape=[
            jax.ShapeDtypeStruct((batch * seq, wc), BF16),
            jax.ShapeDtypeStruct((batch, n_heads, C_HEAD, C_HEAD), F32),
        ],
        scratch_shapes=[pltpu.VMEM((n_heads, C_HEAD, C_HEAD), F32)],
        compiler_params=_cparams("arbitrary", "arbitrary"),
        name="hgrn_prompt",
    )(zc, clb, norm_g)


def _hgrn_sample_kernel(zc_ref, clb_ref, ng_ref, s_ref, y_ref, sout_ref, q_scr, f_scr, k_scr, v_scr, o_scr, *, layer):
    z = zc_ref[...]
    n_rows, n_heads = s_ref.shape[0], s_ref.shape[1]
    wc = z.shape[1] // 4
    D = C_HEAD
    q, fz, iv, g = z[:, :wc], z[:, wc:2 * wc], z[:, 2 * wc:3 * wc], z[:, 3 * wc:]
    log_f, kg = _hgrn_gates(fz, clb_ref[...], layer)
    q_scr[...] = q
    f_scr[...] = jnp.exp(log_f)
    k_scr[...] = kg
    v_scr[...] = iv
    pad = jnp.zeros((8 - n_heads, D), F32)

    def body(i, carry):
        q_r, f_r, k_r, v_r = (t[pl.ds(i, 1), :] for t in (q_scr, f_scr, k_scr, v_scr))
        heads = lambda t: jnp.concatenate([t[:, h * D:(h + 1) * D] for h in range(n_heads)] + [pad], axis=0)
        f_cols = heads(f_r).T
        k_cols = heads(k_r).T
        os_ = []
        for h in range(n_heads):
            sl = slice(h * D, (h + 1) * D)
            s0 = s_ref[i, h]
            qk = jnp.sum(q_r[:, sl] * k_r[:, sl], axis=-1, keepdims=True)
            os_.append(_bdot(q_r[:, sl] * f_r[:, sl], s0) + qk * v_r[:, sl])
            sout_ref[i, h] = s0 * f_cols[:, h:h + 1] + k_cols[:, h:h + 1] * v_r[:, sl]
        o_scr[pl.ds(i, 1), :] = jnp.concatenate(os_, axis=1)
        return carry

    lax.fori_loop(0, n_rows, body, 0)
    y_ref[...] = _hgrn_finish(o_scr[...], g, ng_ref[...]).astype(BF16)


def _hgrn_sample(zc, clb, norm_g, state, n_prompt, layer):
    pc = zc.shape[1]
    wc = pc // 4
    n_s, n_heads = state.shape[0], state.shape[1]
    R = SAMPLE_ROWS
    off = n_prompt // R
    sspec = pl.BlockSpec((R, n_heads, C_HEAD, C_HEAD), lambda i: (i, 0, 0, 0))
    return pl.pallas_call(
        functools.partial(_hgrn_sample_kernel, layer=layer),
        grid=(n_s // R,),
        in_specs=[pl.BlockSpec((R, pc), lambda i: (off + i, 0)), _full(clb.shape), _full(norm_g.shape), sspec],
        out_specs=[pl.BlockSpec((R, wc), lambda i: (i, 0)), sspec],
        out_shape=[jax.ShapeDtypeStruct((n_s, wc), BF16), jax.ShapeDtypeStruct(state.shape, F32)],
        scratch_shapes=[pltpu.VMEM((R, wc), F32)] * 5,
        compiler_params=_cparams("parallel"),
        name="hgrn_sample",
    )(zc, clb, norm_g, state)


def _out_kernel(ya_ref, yb_ref, yc_ref, h_ref, wo_ref, g_ref, b_ref, p_ref, wpg_ref, wpp_ref, hb_ref, res_ref, *, alpha):
    wa, wb = ya_ref.shape[1], yb_ref.shape[1]
    mix = (_dot(ya_ref[...], wo_ref[:wa, :]) + _dot(yb_ref[...], wo_ref[wa:wa + wb, :])
           + _dot(yc_ref[...], wo_ref[wa + wb:, :]))
    h1 = _layer_norm(alpha * h_ref[...] + mix, g_ref[...], b_ref[...], LN_EPS)
    h1b = h1.astype(BF16)
    ple = _sigmoid(_dot(h1b, wpg_ref[...])) * _dot(p_ref[...], wpp_ref[...])
    hb_ref[...] = h1b
    res_ref[...] = alpha * h1 + ple


def _out_proj(ya, yb, yc, h, wo, g, b, p, wpg, wpp, alpha):
    m, d = h.shape
    tm = _pick_tile(m, 320, 16)
    rows = lambda a: pl.BlockSpec((tm, a.shape[1]), lambda i: (i, 0))
    const = lambda a: pl.BlockSpec(a.shape, lambda i: (0, 0), pipeline_mode=pl.Buffered(1))
    return pl.pallas_call(
        functools.partial(_out_kernel, alpha=alpha),
        grid=(m // tm,),
        in_specs=[rows(ya), rows(yb), rows(yc), rows(h), const(wo), const(g), const(b), rows(p), const(wpg), const(wpp)],
        out_specs=[rows(h), rows(h)],
        out_shape=[jax.ShapeDtypeStruct((m, d), BF16), jax.ShapeDtypeStruct((m, d), F32)],
        compiler_params=_cparams("parallel"),
        name="out_proj",
    )(ya, yb, yc, h, wo, g, b, p, wpg, wpp)


def _ffn_kernel(hb_ref, res_ref, wg_ref, wu_ref, wd_ref, g_ref, b_ref, h_ref, hbo_ref, acc_ref):
    f = pl.program_id(1)

    @pl.when(f == 0)
    def _():
        acc_ref[...] = jnp.zeros_like(acc_ref)

    x = hb_ref[...]
    gate = _dot(x, wg_ref[...])
    up = _dot(x, wu_ref[...])
    act = (gate * _sigmoid(gate) * up).astype(BF16)
    acc_ref[...] += _dot(act, wd_ref[...])

    @pl.when(f == pl.num_programs(1) - 1)
    def _():
        h2 = _layer_norm(res_ref[...] + acc_ref[...], g_ref[...], b_ref[...], LN_EPS)
        h_ref[...] = h2
        hbo_ref[...] = h2.astype(BF16)


def _ffn(hb, res, wg, wu, wd, g, b):
    m, d = hb.shape
    dff = wg.shape[1]
    tm = _pick_tile(m, 640, 16)
    tf = _pick_tile(dff, 512, 128)
    return pl.pallas_call(
        _ffn_kernel,
        grid=(m // tm, dff // tf),
        in_specs=[
            pl.BlockSpec((tm, d), lambda i, f: (i, 0)),
            pl.BlockSpec((tm, d), lambda i, f: (i, 0)),
            pl.BlockSpec((d, tf), lambda i, f: (0, f)),
            pl.BlockSpec((d, tf), lambda i, f: (0, f)),
            pl.BlockSpec((tf, d), lambda i, f: (f, 0)),
            pl.BlockSpec((1, d), lambda i, f: (0, 0)),
            pl.BlockSpec((1, d), lambda i, f: (0, 0)),
        ],
        out_specs=[pl.BlockSpec((tm, d), lambda i, f: (i, 0)), pl.BlockSpec((tm, d), lambda i, f: (i, 0))],
        out_shape=[jax.ShapeDtypeStruct((m, d), F32), jax.ShapeDtypeStruct((m, d), BF16)],
        scratch_shapes=[pltpu.VMEM((tm, d), F32)],
        compiler_params=_cparams("parallel", "arbitrary"),
        name="ffn",
    )(hb, res, wg, wu, wd, g, b)


def kernel(x_prompt, x_sample, state_rwkv, state_shift, state_hgrn, p_prompt, p_sample, ln_in_g, ln_in_b, w_in, a_ln_g, a_ln_b, a_ws, a_bs, b_mu, b_w0, b_w_up, b_a0, b_a_up, b_g_up, b_k_k, b_k_a, b_r_k, b_gn_g, b_gn_b, c_lower_bounds, c_norm_g, w_out, ln1_g, ln1_b, w_ffn_gate, w_ffn_up, w_ffn_down, w_ple_gate, w_ple_proj, ln2_g, ln2_b):
    batch, seq, d = x_prompt.shape
    n_s = x_sample.shape[0]
    depth = w_in.shape[0]
    n_p = batch * seq
    wa, wb, wc = a_ln_g.shape[1], b_w0.shape[1], c_norm_g.shape[1]
    pa, pb = 2 * wa, 3 * wb + LORA_W + LORA_A + LORA_G
    alpha = float((2 * depth) ** 0.25)
    assert x_sample.shape[1] == 1 and seq % A_CHUNK == 0 and seq % C_CHUNK == 0 and seq % RWKV_CHUNK == 0
    assert n_s % ROW_BLOCK == 0 and n_p % ROW_BLOCK == 0

    row = lambda t: t.reshape(1, -1)
    ones_bd = (jnp.arange(wb)[:, None] // B_HEAD == jnp.arange(wb)[None, :] // B_HEAD).astype(BF16)

    h, hb = _ln_in(x_prompt.reshape(n_p, d), x_sample.reshape(n_s, d), row(ln_in_g), row(ln_in_b))

    rwkv_p, shift_p, hgrn_p, rwkv_s, shift_s, hgrn_s, sgu_v = [], [], [], [], [], [], []
    for l in range(depth):
        w_l = w_in[l]
        za = _mm(hb, w_l[:, :pa].astype(BF16), "proj_a")
        zb = _mm(hb, w_l[:, pa:pa + pb].astype(BF16), "proj_b")
        zc = _mm(hb, w_l[:, pa + pb:].astype(BF16), "proj_c")

        ya, v_rows = _sgu(za, row(a_ln_g[l]), row(a_ln_b[l]), a_ws[l], a_bs[l].T, n_p)

        rwkv_params = (row(b_mu[l]), row(b_w0[l]), b_w_up[l].astype(BF16), row(b_a0[l]), b_a_up[l].astype(BF16),
                       b_g_up[l].astype(BF16), row(b_k_k[l]), row(b_k_a[l]), row(b_r_k[l]), row(b_gn_g[l]),
                       row(b_gn_b[l]), ones_bd)
        yb_s, r_s = _rwkv_sample(zb, state_shift[l], state_rwkv[l], rwkv_params, n_p)
        yb_p, r_p = _rwkv_prompt(zb, rwkv_params, batch, seq)

        ng = row(c_norm_g[l])
        yc_s, c_s = _hgrn_sample(zc, c_lower_bounds, ng, state_hgrn[l], n_p, l)
        yc_p, c_p = _hgrn_prompt(zc, c_lower_bounds, ng, batch, seq, l)

        p_all = jnp.concatenate([p_prompt[l].reshape(n_p, -1), p_sample[l].reshape(n_s, -1)], axis=0).astype(BF16)
        hb, res = _out_proj(ya, jnp.concatenate([yb_p, yb_s], axis=0), jnp.concatenate([yc_p, yc_s], axis=0), h,
                            w_out[l].astype(BF16), row(ln1_g[l]), row(ln1_b[l]), p_all,
                            w_ple_gate[l].astype(BF16), w_ple_proj[l].astype(BF16), alpha)
        h, hb = _ffn(hb, res, w_ffn_gate[l].astype(BF16), w_ffn_up[l].astype(BF16), w_ffn_down[l].astype(BF16),
                     row(ln2_g[l]), row(ln2_b[l]))

        rwkv_p.append(r_p)
        shift_p.append(zb[:n_p].reshape(batch, seq, pb)[:, -1])
        hgrn_p.append(c_p)
        rwkv_s.append(r_s)
        shift_s.append(zb[n_p:])
        hgrn_s.append(c_s)
        sgu_v.append(v_rows.reshape(n_s, 1, wa))

    return (h[:n_p].reshape(batch, seq, d), h[n_p:].reshape(n_s, 1, d), jnp.stack(rwkv_p), jnp.stack(shift_p),
            jnp.stack(hgrn_p), jnp.stack(rwkv_s), jnp.stack(shift_s), jnp.stack(hgrn_s), jnp.stack(sgu_v))
```

```python
import functools

import jax
import jax.numpy as jnp
from jax import lax
from jax.experimental import pallas as pl
from jax.experimental.pallas import tpu as pltpu

F32 = jnp.float32
BF16 = jnp.bfloat16
HIGHEST = lax.Precision.HIGHEST

A_GROUPS = 4
A_CHUNK = 128
B_HEAD = 64
LORA_W, LORA_A, LORA_G = 64, 64, 128
C_HEAD = 128
C_CHUNK = 128
LN_EPS = 1e-5
B_GN_EPS = 1e-5 * B_HEAD
RMS_EPS = 1e-6

RWKV_CHUNK = 64
HGRN_SUB = 16
SAMPLE_ROWS = 16
ROW_BLOCK = 128
MXU_TILE = 256
VMEM_LIMIT_BYTES = 56 * 1024 * 1024


def _cparams(*sem):
    return pltpu.CompilerParams(dimension_semantics=sem, vmem_limit_bytes=VMEM_LIMIT_BYTES)


def _pick_tile(n, target, align):
    best = None
    for t in range(align, min(n, target) + 1, align):
        if n % t == 0:
            best = t
    assert best is not None, (n, target, align)
    return best


def _dot(a, b, prec=None):
    return jnp.dot(a, b, precision=prec, preferred_element_type=F32)


def _dot_nt(a, b, prec=None):
    return lax.dot_general(a, b, (((1,), (1,)), ((), ())), precision=prec, preferred_element_type=F32)


def _dot_tn(a, b, prec=None):
    return lax.dot_general(a, b, (((0,), (0,)), ((), ())), precision=prec, preferred_element_type=F32)


def _bdot(a, b):
    return jnp.dot(a.astype(BF16), b.astype(BF16), preferred_element_type=F32)


def _bdot_nt(a, b):
    return _dot_nt(a.astype(BF16), b.astype(BF16))


def _layer_norm(x, g, b, eps):
    mu = jnp.mean(x, axis=-1, keepdims=True)
    xc = x - mu
    var = jnp.mean(xc * xc, axis=-1, keepdims=True)
    return xc * lax.rsqrt(var + eps) * g + b


def _gelu(x):
    return 0.5 * x * (1.0 + lax.erf(x * 0.7071067811865476))


def _sigmoid(x):
    return 1.0 / (1.0 + jnp.exp(-x))


def _log_sigmoid(x):
    return jnp.minimum(x, 0.0) - jnp.log1p(jnp.exp(-jnp.abs(x)))


def _split_bf16(x):
    hi = x.astype(BF16).astype(F32)
    return hi, x - hi


def _seg_sum(x, ones_bd):
    rows, width = x.shape
    t = ones_bd.shape[0]
    nb = width // t
    hi = x.astype(BF16)
    lo = (x - hi.astype(F32)).astype(BF16)
    parts = [p[:, j * t:(j + 1) * t] for p in (hi, lo) for j in range(nb)]
    r = _dot(jnp.concatenate(parts, axis=0), ones_bd)
    return jnp.concatenate(
        [r[j * rows:(j + 1) * rows] + r[(nb + j) * rows:(nb + j + 1) * rows] for j in range(nb)], axis=1)


def _cumsum_rows(x):
    rows, width = x.shape
    hi = x.astype(BF16)
    r1 = x - hi.astype(F32)
    mid = r1.astype(BF16)
    lo = (r1 - mid.astype(F32)).astype(BF16)
    tri = lax.broadcasted_iota(jnp.int32, (rows, rows), 0) >= lax.broadcasted_iota(jnp.int32, (rows, rows), 1)
    c = _dot(jnp.where(tri, 1.0, 0.0).astype(BF16), jnp.concatenate([hi, mid, lo], axis=1))
    return c[:, :width] + c[:, width:2 * width] + c[:, 2 * width:]


def _layer_block(arr, layer):
    nd = arr.ndim - 1
    return pl.BlockSpec((None,) + arr.shape[1:], lambda *_: (layer,) + (0,) * nd)


def _whole(arr):
    nd = arr.ndim
    return pl.BlockSpec(arr.shape, lambda *_: (0,) * nd)


def _ln_in_kernel(xp_ref, xs_ref, g_ref, b_ref, h_ref, hb_ref, *, n_prompt_blocks):
    i = pl.program_id(0)

    def emit(x):
        h = _layer_norm(x, g_ref[...], b_ref[...], LN_EPS)
        h_ref[...] = h
        hb_ref[...] = h.astype(BF16)

    @pl.when(i < n_prompt_blocks)
    def _():
        emit(xp_ref[...])

    @pl.when(i >= n_prompt_blocks)
    def _():
        emit(xs_ref[...])


def _ln_in(xp, xs, g, b):
    n_p, d = xp.shape
    n_s = xs.shape[0]
    npb, nsb = n_p // ROW_BLOCK, n_s // ROW_BLOCK
    m = n_p + n_s
    return pl.pallas_call(
        functools.partial(_ln_in_kernel, n_prompt_blocks=npb),
        grid=(npb + nsb,),
        in_specs=[
            pl.BlockSpec((ROW_BLOCK, d), lambda i: (jnp.minimum(i, npb - 1), 0)),
            pl.BlockSpec((ROW_BLOCK, d), lambda i: (jnp.maximum(i - npb, 0), 0)),
            pl.BlockSpec((1, d), lambda i: (0, 0)),
            pl.BlockSpec((1, d), lambda i: (0, 0)),
        ],
        out_specs=[pl.BlockSpec((ROW_BLOCK, d), lambda i: (i, 0)), pl.BlockSpec((ROW_BLOCK, d), lambda i: (i, 0))],
        out_shape=[jax.ShapeDtypeStruct((m, d), F32), jax.ShapeDtypeStruct((m, d), BF16)],
        compiler_params=_cparams("parallel"),
        name="ln_in",
    )(xp, xs, g, b)


def _mm_kernel(x_ref, w_ref, o_ref):
    o_ref[...] = jnp.dot(x_ref[...], w_ref[...], preferred_element_type=F32)


def _mm(xb, w, layer, col0, n, name):
    m, k = xb.shape
    tm = _pick_tile(m, 640, 16)
    tn = _pick_tile(n, 512, 128)
    assert col0 % tn == 0
    j0 = col0 // tn
    return pl.pallas_call(
        _mm_kernel,
        grid=(m // tm, n // tn),
        in_specs=[pl.BlockSpec((tm, k), lambda i, j: (i, 0)),
                  pl.BlockSpec((None, k, tn), lambda i, j: (layer, 0, j0 + j))],
        out_specs=pl.BlockSpec((tm, tn), lambda i, j: (i, j)),
        out_shape=jax.ShapeDtypeStruct((m, n), F32),
        compiler_params=_cparams("parallel", "parallel"),
        name=name,
    )(xb, w)


def _sgu_kernel(u_ref, v_ref, ln_ref, ws_ref, bst_ref, y_ref, vn_ref, *, n_prompt_blocks):
    i = pl.program_id(0)
    gd = u_ref.shape[1] // A_GROUPS
    u = _gelu(u_ref[...])
    v = _gelu(v_ref[...])
    vn = [
        _layer_norm(v[:, g * gd:(g + 1) * gd], ln_ref[0:1, g * gd:(g + 1) * gd], ln_ref[1:2, g * gd:(g + 1) * gd], LN_EPS)
        for g in range(A_GROUPS)
    ]

    @pl.when(i < n_prompt_blocks)
    def _():
        row = lax.broadcasted_iota(jnp.int32, (A_CHUNK, A_CHUNK), 0)
        col = lax.broadcasted_iota(jnp.int32, (A_CHUNK, A_CHUNK), 1)
        for g in range(A_GROUPS):
            w_causal = jnp.where(row >= col, ws_ref[g], 0.0)
            mixed = _bdot(w_causal, vn[g]) + bst_ref[:, g:g + 1]
            y_ref[:, g * gd:(g + 1) * gd] = (u[:, g * gd:(g + 1) * gd] * mixed).astype(BF16)

    @pl.when(i >= n_prompt_blocks)
    def _():
        for g in range(A_GROUPS):
            mixed = vn[g] * ws_ref[g, 0:1, 0:1] + bst_ref[0:1, g:g + 1]
            y_ref[:, g * gd:(g + 1) * gd] = (u[:, g * gd:(g + 1) * gd] * mixed).astype(BF16)
            vn_ref[:, g * gd:(g + 1) * gd] = vn[g]


def _sgu(za, ln, ws, bst, layer, n_prompt):
    m = za.shape[0]
    wa = za.shape[1] // 2
    npb = n_prompt // A_CHUNK
    nb = m // A_CHUNK
    n_s = m - n_prompt
    return pl.pallas_call(
        functools.partial(_sgu_kernel, n_prompt_blocks=npb),
        grid=(nb,),
        in_specs=[
            pl.BlockSpec((A_CHUNK, wa), lambda i: (i, 0)),
            pl.BlockSpec((A_CHUNK, wa), lambda i: (i, 1)),
            _layer_block(ln, layer), _layer_block(ws, layer), _layer_block(bst, layer),
        ],
        out_specs=[
            pl.BlockSpec((A_CHUNK, wa), lambda i: (i, 0)),
            pl.BlockSpec((A_CHUNK, wa), lambda i: (jnp.maximum(i - npb, 0), 0)),
        ],
        out_shape=[jax.ShapeDtypeStruct((m, wa), BF16), jax.ShapeDtypeStruct((n_s, wa), F32)],
        compiler_params=_cparams("arbitrary"),
        name="sgu",
    )(za, za, ln, ws, bst)


_V_W0, _V_A0, _V_KK, _V_KA, _V_RK, _V_GNG, _V_GNB = range(7)


def _rwkv_prep(zb, prev, mu, vec, wup, aup, gup, ones_bd):
    wb = vec.shape[1]
    row = lambda j: vec[j:j + 1, :]
    xs = zb + mu * (prev - zb)
    r, k, v = xs[:, :wb], xs[:, wb:2 * wb], xs[:, 2 * wb:3 * wb]
    o4, o5 = 3 * wb + LORA_W, 3 * wb + LORA_W + LORA_A
    wd, ad, gd = xs[:, 3 * wb:o4], xs[:, o4:o5], xs[:, o5:]
    w = row(_V_W0) + _bdot(jnp.tanh(wd), wup)
    softplus_neg_w = jnp.maximum(-w, 0.0) + jnp.log1p(jnp.exp(-jnp.abs(w)))
    log_decay = -jnp.exp(-softplus_neg_w - 0.5)
    a = _sigmoid(row(_V_A0) + _bdot(ad, aup))
    g = _bdot(_sigmoid(gd), gup)
    kk = k * row(_V_KK)
    kk = kk / jnp.maximum(jnp.sqrt(_seg_sum(kk * kk, ones_bd)), 1e-12)
    kd = k * (1.0 + (a - 1.0) * row(_V_KA))
    bonus = _seg_sum(r * kd * row(_V_RK), ones_bd) * v
    return r, log_decay, kd, v, kk, a, g, bonus


def _rwkv_finish(y, bonus, g, vec, ones_bd):
    mean = _seg_sum(y, ones_bd) * (1.0 / B_HEAD)
    yc = y - mean
    var = _seg_sum(yc * yc, ones_bd) * (1.0 / B_HEAD)
    return (yc * lax.rsqrt(var + B_GN_EPS) * vec[_V_GNG:_V_GNG + 1, :] + vec[_V_GNB:_V_GNB + 1, :] + bonus) * g


def _rwkv_prompt_kernel(zb_ref, ys_ref, mu_ref, vec_ref, wup_ref, aup_ref, gup_ref, ones_ref,
                        y_ref, sout_ref, state_scr, prev_scr, *, n_chunks, n_prompt_steps):
    i = pl.program_id(0)

    @pl.when(i >= n_prompt_steps)
    def _():
        y_ref[...] = ys_ref[...]

    @pl.when(i < n_prompt_steps)
    def _():
        c_idx = lax.rem(i, n_chunks)

        @pl.when(c_idx == 0)
        def _():
            state_scr[...] = jnp.zeros_like(state_scr)
            prev_scr[...] = jnp.zeros_like(prev_scr)

        zb = zb_ref[...]
        C = zb.shape[0]
        n_pairs = state_scr.shape[0]
        N = B_HEAD
        P = 2 * N
        ones_bd = ones_ref[...]
        vec = vec_ref[...]
        row1 = lax.broadcasted_iota(jnp.int32, (C, 1), 0)
        prev = jnp.where(row1 == 0, prev_scr[...], pltpu.roll(zb, 1, 0))
        prev_scr[...] = zb[C - 1:C, :]
        r, lw, kd, v, kk, a, g, bonus = _rwkv_prep(
            zb, prev, mu_ref[...], vec, wup_ref[...], aup_ref[...], gup_ref[...], ones_bd)

        cum = _cumsum_rows(lw)
        m = cum[C // 2 - 1:C // 2, :]
        cend = cum[C - 1:C, :]
        e_pos = jnp.exp(cum - m)
        e_neg = jnp.exp(m - cum)
        e_prev = jnp.exp(cum - lw - m)
        e_m = jnp.exp(m)
        e_end = jnp.exp(cend - m)
        rp = r * e_pos
        kkp = kk * e_prev
        kp = kd * e_neg
        bp = kk * a * e_neg
        rhat = rp * e_m
        kkhat = kkp * e_m
        ktil = kp * e_end
        btil = bp * e_end
        e_cend = e_end * e_m

        lane = lax.broadcasted_iota(jnp.int32, (C, P), 1)
        rowc = lax.broadcasted_iota(jnp.int32, (C, P), 0)
        lo = lane < N
        col_in = jnp.where(lo, lane, lane - N)
        incl2 = rowc >= col_in
        strict2 = rowc > col_in
        lo2 = lax.broadcasted_iota(jnp.int32, (2 * C, P), 1) < N
        bd_mask = (lax.broadcasted_iota(jnp.int32, (P, P), 0) < N) == (lax.broadcasted_iota(jnp.int32, (P, P), 1) < N)
        pairs = [slice(p * P, (p + 1) * P) for p in range(n_pairs)]

        ar1, ar2, n_bd, xs, v_b = [], [], [], [], []
        for ps in pairs:
            q_pair = jnp.concatenate([rp[:, ps], kkp[:, ps]], axis=0)
            q1 = jnp.where(lo2, q_pair, 0.0).astype(BF16)
            q2 = jnp.where(lo2, 0.0, q_pair).astype(BF16)
            kp_b, bp_b = kp[:, ps].astype(BF16), bp[:, ps].astype(BF16)
            sc1 = _dot_nt(q1, jnp.concatenate([bp_b, kp_b], axis=0))
            sc2 = _dot_nt(q2, jnp.concatenate([kp_b, bp_b], axis=0))
            ar1.append(jnp.where(incl2, sc1[:C], 0.0).astype(BF16))
            ar2.append(jnp.where(incl2, sc2[:C], 0.0).astype(BF16))
            kn1 = jnp.where(strict2, sc1[C:], 0.0)
            kn2 = jnp.where(strict2, sc2[C:], 0.0)
            n_bd.append(jnp.concatenate([jnp.where(lo, kn1, 0.0), jnp.where(lo, 0.0, kn2)], axis=0).astype(BF16))
            akk = jnp.concatenate([jnp.where(lo, 0.0, kn1), jnp.where(lo, kn2, 0.0)], axis=0).astype(BF16)
            v_pair = v[:, ps]
            v_b.append(v_pair.astype(BF16))
            v_sw = pltpu.roll(v_pair, N, 1).astype(BF16)
            av = _dot(akk, jnp.concatenate([v_sw, v_sw], axis=0))
            kkh = kkhat[:, ps]
            xs.append(jnp.concatenate([jnp.where(lo, kkh, av[:C]), jnp.where(lo, av[C:], kkh)], axis=0))

        xs = [x - _dot(nb, x.astype(BF16)) for nb, x in zip(n_bd, xs)]
        pw = n_bd
        span = 2
        while span < C:
            pw = [_dot(t, t).astype(BF16) for t in pw]
            xs = [x + _dot(t, x.astype(BF16)) for t, x in zip(pw, xs)]
            span *= 2

        ys = []
        for p, ps in enumerate(pairs):
            x = xs[p]
            w1 = jnp.where(lo, x[:C], x[C:])
            w2 = pltpu.roll(jnp.where(lo, x[C:], x[:C]), N, 1)
            s_bd = state_scr[p]
            qs = _dot_nt(jnp.concatenate([w1, rhat[:, ps]], axis=0).astype(BF16), s_bd.astype(BF16))
            u = -(qs[:C] + w2)
            u_hi, u_lo = _split_bf16(u)
            u_b = u_hi.astype(BF16)
            y1 = _dot(ar1[p], jnp.concatenate([u_b, v_b[p]], axis=0))
            y2 = _dot(ar2[p], jnp.concatenate([v_b[p], u_b], axis=0))
            ys.append(qs[C:] + jnp.where(lo, y1, y2))
            kb = jnp.concatenate([ktil[:, ps], btil[:, ps], btil[:, ps]], axis=0).astype(BF16)
            vu = jnp.concatenate([v_b[p], u_b, u_lo.astype(BF16)], axis=0)
            state_scr[p] = jnp.where(bd_mask, s_bd * e_cend[:, ps] + _dot_tn(vu, kb), 0.0)

        y = jnp.concatenate(ys, axis=1)
        y_ref[...] = _rwkv_finish(y, bonus, g, vec, ones_bd).astype(BF16)

        @pl.when(c_idx == n_chunks - 1)
        def _():
            for p in range(n_pairs):
                s_bd = state_scr[p]
                sout_ref[0, 2 * p] = s_bd[:N, :N]
                sout_ref[0, 2 * p + 1] = s_bd[N:, N:]


def _rwkv_prompt(zb, y_sample, params, layer, batch, seq):
    mu, vec, wup, aup, gup, ones_bd = params
    pb = zb.shape[1]
    wb = vec.shape[2]
    n_heads = wb // B_HEAD
    C = RWKV_CHUNK
    nc = seq // C
    n_steps = batch * nc
    n_s = y_sample.shape[0]
    m = batch * seq + n_s
    return pl.pallas_call(
        functools.partial(_rwkv_prompt_kernel, n_chunks=nc, n_prompt_steps=n_steps),
        grid=(n_steps + n_s // C,),
        in_specs=[
            pl.BlockSpec((C, pb), lambda i: (jnp.minimum(i, n_steps - 1), 0)),
            pl.BlockSpec((C, wb), lambda i: (jnp.maximum(i - n_steps, 0), 0)),
            _layer_block(mu, layer), _layer_block(vec, layer), _layer_block(wup, layer),
            _layer_block(aup, layer), _layer_block(gup, layer), _whole(ones_bd),
        ],
        out_specs=[
            pl.BlockSpec((C, wb), lambda i: (i, 0)),
            pl.BlockSpec((1, n_heads, B_HEAD, B_HEAD), lambda i: (jnp.minimum(i // nc, batch - 1), 0, 0, 0)),
        ],
        out_shape=[
            jax.ShapeDtypeStruct((m, wb), BF16),
            jax.ShapeDtypeStruct((batch, n_heads, B_HEAD, B_HEAD), F32),
        ],
        scratch_shapes=[pltpu.VMEM((n_heads // 2, 2 * B_HEAD, 2 * B_HEAD), F32), pltpu.VMEM((1, pb), F32)],
        compiler_params=_cparams("arbitrary"),
        name="rwkv_prompt",
    )(zb, y_sample, mu, vec, wup, aup, gup, ones_bd)


def _rwkv_sample_kernel(zb_ref, shift_ref, mu_ref, vec_ref, wup_ref, aup_ref, gup_ref, ones_ref, s_ref, buf_ref,
                        y_ref, sout_ref, kk_scr, wr_scr, b_scr, kd_scr, v_scr, w_scr, br_scr, kr_scr, y_scr):
    del buf_ref
    n_rows, n_heads = s_ref.shape[0], s_ref.shape[1]
    N = B_HEAD
    ones_bd = ones_ref[...]
    vec = vec_ref[...]
    r, lw, kd, v, kk, a, g, bonus = _rwkv_prep(
        zb_ref[...], shift_ref[...], mu_ref[...], vec, wup_ref[...], aup_ref[...], gup_ref[...], ones_bd)
    w = jnp.exp(lw)
    b = kk * a
    kk_scr[...] = kk
    wr_scr[...] = w * r
    b_scr[...] = b
    kd_scr[...] = kd
    v_scr[...] = v
    w_scr[...] = w
    br_scr[...] = _seg_sum(b * r, ones_bd)
    kr_scr[...] = _seg_sum(kd * r, ones_bd)
    width = kk.shape[1]
    heads = [slice(h * N, (h + 1) * N) for h in range(n_heads)]

    def body(i, carry):
        ld = lambda ref: ref[pl.ds(i, 1), :]
        kk_hi, kk_lo = _split_bf16(ld(kk_scr))
        wr_hi, wr_lo = _split_bf16(ld(wr_scr))
        lhs = jnp.concatenate([kk_hi, kk_lo, wr_hi, wr_lo, jnp.zeros((12, width), F32)], axis=0).astype(BF16)
        s_kk, yq = [], []
        for h, sl in enumerate(heads):
            rr = _dot_nt(lhs[:, sl], s_ref[i, h].astype(BF16))
            s_kk.append(rr[0:1] + rr[1:2])
            yq.append(rr[2:3] + rr[3:4])
        s_kk = jnp.concatenate(s_kk, axis=1)
        v_r = ld(v_scr)
        y_scr[pl.ds(i, 1), :] = jnp.concatenate(yq, axis=1) - s_kk * ld(br_scr) + v_r * ld(kr_scr)
        nk_hi, nk_lo = _split_bf16(-s_kk)
        v_hi, v_lo = _split_bf16(v_r)
        b_hi, b_lo = _split_bf16(ld(b_scr))
        kd_hi, kd_lo = _split_bf16(ld(kd_scr))
        pad = jnp.zeros((10, width), F32)
        left = jnp.concatenate([nk_hi, nk_hi, nk_lo, v_hi, v_hi, v_lo, pad], axis=0).astype(BF16)
        right = jnp.concatenate([b_hi, b_lo, b_hi, kd_hi, kd_lo, kd_hi, pad], axis=0).astype(BF16)
        w_r = ld(w_scr)
        for h, sl in enumerate(heads):
            sout_ref[i, h] = s_ref[i, h] * w_r[:, sl] + _dot_tn(left[:, sl], right[:, sl])
        return carry

    lax.fori_loop(0, n_rows, body, 0)
    y_ref[...] = _rwkv_finish(y_scr[...], bonus, g, vec, ones_bd).astype(BF16)


def _rwkv_sample(zb, shift, state, out_buf, params, layer, n_prompt):
    mu, vec, wup, aup, gup, ones_bd = params
    pb = zb.shape[1]
    wb = vec.shape[2]
    n_s, n_heads = state.shape[1], state.shape[2]
    R = SAMPLE_ROWS
    off = n_prompt // R
    sspec = pl.BlockSpec((None, R, n_heads, B_HEAD, B_HEAD), lambda i: (layer, i, 0, 0, 0))
    return pl.pallas_call(
        _rwkv_sample_kernel,
        grid=(n_s // R,),
        in_specs=[
            pl.BlockSpec((R, pb), lambda i: (off + i, 0)),
            pl.BlockSpec((None, R, pb), lambda i: (layer, i, 0)),
            _layer_block(mu, layer), _layer_block(vec, layer), _layer_block(wup, layer),
            _layer_block(aup, layer), _layer_block(gup, layer), _whole(ones_bd),
            sspec, pl.BlockSpec(memory_space=pl.ANY),
        ],
        out_specs=[pl.BlockSpec((R, wb), lambda i: (i, 0)), sspec],
        out_shape=[jax.ShapeDtypeStruct((n_s, wb), BF16), jax.ShapeDtypeStruct(state.shape, F32)],
        input_output_aliases={9: 1},
        scratch_shapes=[pltpu.VMEM((R, wb), F32)] * 9,
        compiler_params=_cparams("parallel"),
        name="rwkv_sample",
    )(zb, shift, mu, vec, wup, aup, gup, ones_bd, state, out_buf)


def _hgrn_gates(fz, clb, layer):
    ls_pos = _log_sigmoid(fz)
    if layer == 0:
        return ls_pos, _sigmoid(-fz)
    e = jnp.exp(clb - jnp.max(clb, axis=0, keepdims=True))
    sm = e / jnp.sum(e, axis=0, keepdims=True)
    lb = jnp.sum(sm[1:layer + 1], axis=0, keepdims=True)
    x1 = ls_pos
    x2 = jnp.log(lb) + _log_sigmoid(-fz)
    log_f = jnp.maximum(x1, x2) + jnp.log1p(jnp.exp(-jnp.abs(x1 - x2)))
    return log_f, (1.0 - lb) * _sigmoid(-fz)


def _hgrn_finish(o, g, norm_g):
    on = o * lax.rsqrt(jnp.mean(o * o, axis=-1, keepdims=True) + RMS_EPS) * norm_g
    return on * (g * _sigmoid(g))


def _hgrn_prompt_kernel(zc_ref, ys_ref, clb_ref, ng_ref, y_ref, sout_ref, state_scr, *, layer, n_chunks, n_prompt_steps):
    i = pl.program_id(0)

    @pl.when(i >= n_prompt_steps)
    def _():
        y_ref[...] = ys_ref[...]

    @pl.when(i < n_prompt_steps)
    def _():
        c_idx = lax.rem(i, n_chunks)

        @pl.when(c_idx == 0)
        def _():
            state_scr[...] = jnp.zeros_like(state_scr)

        z = zc_ref[...]
        C = z.shape[0]
        wc = z.shape[1] // 4
        n_heads = wc // C_HEAD
        D = C_HEAD
        SUB = HGRN_SUB
        nsub = C // SUB
        q, fz, iv, g = z[:, :wc], z[:, wc:2 * wc], z[:, 2 * wc:3 * wc], z[:, 3 * wc:]
        log_f, kg = _hgrn_gates(fz, clb_ref[...], layer)
        bcum = _cumsum_rows(log_f)
        e_b = jnp.exp(bcum)
        b_end = bcum[C - 1:C, :]
        e_end = jnp.exp(b_end)
        k_hat = kg * jnp.exp(b_end - bcum)
        t_in_sub = lax.broadcasted_iota(jnp.int32, (nsub, SUB, 1), 1)

        outs = []
        for h in range(n_heads):
            sl = slice(h * D, (h + 1) * D)
            qh, kh, vh, bh = q[:, sl], kg[:, sl], iv[:, sl], bcum[:, sl]
            st = state_scr[h]
            o = _bdot_nt(qh * e_b[:, sl], st)
            q3, k3, v3, b3 = (t.reshape(nsub, SUB, D) for t in (qh, kh, vh, bh))
            od = jnp.zeros((nsub, SUB, D), F32)
            for j in range(SUB):
                dec = jnp.exp(jnp.minimum(b3 - b3[:, j:j + 1, :], 0.0))
                att = jnp.sum(q3 * k3[:, j:j + 1, :] * dec, axis=-1, keepdims=True)
                att = jnp.where(t_in_sub >= j, att, 0.0)
                od = od + att * v3[:, j:j + 1, :]
            o = o + od.reshape(C, D)
            off_rows = [jnp.zeros((SUB, D), F32)]
            for s in range(1, nsub):
                lo = s * SUB
                b_bound = bh[lo - 1:lo, :]
                qi = qh[lo:lo + SUB, :] * jnp.exp(bh[lo:lo + SUB, :] - b_bound)
                ki = kh[:lo, :] * jnp.exp(b_bound - bh[:lo, :])
                att = _bdot_nt(qi, ki)
                off_rows.append(_bdot(att, vh[:lo, :]))
            outs.append(o + jnp.concatenate(off_rows, axis=0))
            vh_hi, vh_lo = _split_bf16(vh)
            kh_hi, kh_lo = _split_bf16(k_hat[:, sl])
            upd = _dot_tn(jnp.concatenate([vh_hi, vh_hi, vh_lo], axis=0).astype(BF16),
                          jnp.concatenate([kh_hi, kh_lo, kh_hi], axis=0).astype(BF16))
            state_scr[h] = st * e_end[:, sl] + upd

        o_all = jnp.concatenate(outs, axis=1)
        y_ref[...] = _hgrn_finish(o_all, g, ng_ref[...]).astype(BF16)

        @pl.when(c_idx == n_chunks - 1)
        def _():
            for h in range(n_heads):
                sout_ref[0, h] = state_scr[h].T


def _hgrn_prompt(zc, y_sample, clb, norm_g, layer, batch, seq):
    pc = zc.shape[1]
    wc = pc // 4
    n_heads = wc // C_HEAD
    C = C_CHUNK
    nc = seq // C
    n_steps = batch * nc
    n_s = y_sample.shape[0]
    m = batch * seq + n_s
    return pl.pallas_call(
        functools.partial(_hgrn_prompt_kernel, layer=layer, n_chunks=nc, n_prompt_steps=n_steps),
        grid=(n_steps + n_s // C,),
        in_specs=[
            pl.BlockSpec((C, pc), lambda i: (jnp.minimum(i, n_steps - 1), 0)),
            pl.BlockSpec((C, wc), lambda i: (jnp.maximum(i - n_steps, 0), 0)),
            _whole(clb), _layer_block(norm_g, layer),
        ],
        out_specs=[
            pl.BlockSpec((C, wc), lambda i: (i, 0)),
            pl.BlockSpec((1, n_heads, C_HEAD, C_HEAD), lambda i: (jnp.minimum(i // nc, batch - 1), 0, 0, 0)),
        ],
        out_shape=[
            jax.ShapeDtypeStruct((m, wc), BF16),
            jax.ShapeDtypeStruct((batch, n_heads, C_HEAD, C_HEAD), F32),
        ],
        scratch_shapes=[pltpu.VMEM((n_heads, C_HEAD, C_HEAD), F32)],
        compiler_params=_cparams("arbitrary"),
        name="hgrn_prompt",
    )(zc, y_sample, clb, norm_g)


def _hgrn_sample_kernel(zc_ref, clb_ref, ng_ref, s_ref, buf_ref, y_ref, sout_ref,
                        q_scr, f_scr, k_scr, v_scr, o_scr, *, layer):
    del buf_ref
    z = zc_ref[...]
    n_rows, n_heads = s_ref.shape[0], s_ref.shape[1]
    wc = z.shape[1] // 4
    D = C_HEAD
    q, fz, iv, g = z[:, :wc], z[:, wc:2 * wc], z[:, 2 * wc:3 * wc], z[:, 3 * wc:]
    log_f, kg = _hgrn_gates(fz, clb_ref[...], layer)
    q_scr[...] = q
    f_scr[...] = jnp.exp(log_f)
    k_scr[...] = kg
    v_scr[...] = iv
    pad = jnp.zeros((8 - n_heads, D), F32)

    def body(i, carry):
        q_r, f_r, k_r, v_r = (t[pl.ds(i, 1), :] for t in (q_scr, f_scr, k_scr, v_scr))
        heads = lambda t: jnp.concatenate([t[:, h * D:(h + 1) * D] for h in range(n_heads)] + [pad], axis=0)
        f_cols = heads(f_r).T
        k_cols = heads(k_r).T
        os_ = []
        for h in range(n_heads):
            sl = slice(h * D, (h + 1) * D)
            s0 = s_ref[i, h]
            qk = jnp.sum(q_r[:, sl] * k_r[:, sl], axis=-1, keepdims=True)
            os_.append(_bdot(q_r[:, sl] * f_r[:, sl], s0) + qk * v_r[:, sl])
            sout_ref[i, h] = s0 * f_cols[:, h:h + 1] + k_cols[:, h:h + 1] * v_r[:, sl]
        o_scr[pl.ds(i, 1), :] = jnp.concatenate(os_, axis=1)
        return carry

    lax.fori_loop(0, n_rows, body, 0)
    y_ref[...] = _hgrn_finish(o_scr[...], g, ng_ref[...]).astype(BF16)


def _hgrn_sample(zc, clb, norm_g, state, out_buf, layer, n_prompt):
    pc = zc.shape[1]
    wc = pc // 4
    n_s, n_heads = state.shape[1], state.shape[2]
    R = SAMPLE_ROWS
    off = n_prompt // R
    sspec = pl.BlockSpec((None, R, n_heads, C_HEAD, C_HEAD), lambda i: (layer, i, 0, 0, 0))
    return pl.pallas_call(
        functools.partial(_hgrn_sample_kernel, layer=layer),
        grid=(n_s // R,),
        in_specs=[pl.BlockSpec((R, pc), lambda i: (off + i, 0)), _whole(clb), _layer_block(norm_g, layer),
                  sspec, pl.BlockSpec(memory_space=pl.ANY)],
        out_specs=[pl.BlockSpec((R, wc), lambda i: (i, 0)), sspec],
        out_shape=[jax.ShapeDtypeStruct((n_s, wc), BF16), jax.ShapeDtypeStruct(state.shape, F32)],
        input_output_aliases={4: 1},
        scratch_shapes=[pltpu.VMEM((R, wc), F32)] * 5,
        compiler_params=_cparams("parallel"),
        name="hgrn_sample",
    )(zc, clb, norm_g, state, out_buf)


def _out_kernel(ya_ref, yb_ref, yc_ref, h_ref, wo_ref, ln_ref, p_ref, wpg_ref, wpp_ref, hb_ref, res_ref, *, alpha):
    wa, wb = ya_ref.shape[1], yb_ref.shape[1]
    mix = (_dot(ya_ref[...], wo_ref[:wa, :]) + _dot(yb_ref[...], wo_ref[wa:wa + wb, :])
           + _dot(yc_ref[...], wo_ref[wa + wb:, :]))
    h1 = _layer_norm(alpha * h_ref[...] + mix, ln_ref[0:1, :], ln_ref[1:2, :], LN_EPS)
    h1b = h1.astype(BF16)
    ple = _sigmoid(_dot(h1b, wpg_ref[...])) * _dot(p_ref[...], wpp_ref[...])
    hb_ref[...] = h1b
    res_ref[...] = alpha * h1 + ple


def _out_proj(ya, yb, yc, h, wo, ln, p, wpg, wpp, layer, alpha):
    m, d = h.shape
    tm = _pick_tile(m, 320, 16)
    rows = lambda a: pl.BlockSpec((tm, a.shape[-1]), lambda i: (i, 0))
    const = lambda a: pl.BlockSpec((None,) + a.shape[1:], lambda i: (layer, 0, 0), pipeline_mode=pl.Buffered(1))
    return pl.pallas_call(
        functools.partial(_out_kernel, alpha=alpha),
        grid=(m // tm,),
        in_specs=[rows(ya), rows(yb), rows(yc), rows(h), const(wo), const(ln),
                  pl.BlockSpec((None, tm, p.shape[-1]), lambda i: (layer, i, 0)), const(wpg), const(wpp)],
        out_specs=[rows(h), rows(h)],
        out_shape=[jax.ShapeDtypeStruct((m, d), BF16), jax.ShapeDtypeStruct((m, d), F32)],
        compiler_params=_cparams("parallel"),
        name="out_proj",
    )(ya, yb, yc, h, wo, ln, p, wpg, wpp)


def _ffn_kernel(hb_ref, res_ref, wg_ref, wu_ref, wd_ref, ln_ref, h_ref, hbo_ref, acc_ref):
    f = pl.program_id(1)

    @pl.when(f == 0)
    def _():
        acc_ref[...] = jnp.zeros_like(acc_ref)

    x = hb_ref[...]
    gate = _dot(x, wg_ref[...])
    up = _dot(x, wu_ref[...])
    act = (gate * _sigmoid(gate) * up).astype(BF16)
    acc_ref[...] += _dot(act, wd_ref[...])

    @pl.when(f == pl.num_programs(1) - 1)
    def _():
        h2 = _layer_norm(res_ref[...] + acc_ref[...], ln_ref[0:1, :], ln_ref[1:2, :], LN_EPS)
        h_ref[...] = h2
        hbo_ref[...] = h2.astype(BF16)


def _ffn(hb, res, wg, wu, wd, ln, layer):
    m, d = hb.shape
    dff = wg.shape[2]
    tm = _pick_tile(m, 640, 16)
    tf = _pick_tile(dff, 512, 128)
    return pl.pallas_call(
        _ffn_kernel,
        grid=(m // tm, dff // tf),
        in_specs=[
            pl.BlockSpec((tm, d), lambda i, f: (i, 0)),
            pl.BlockSpec((tm, d), lambda i, f: (i, 0)),
            pl.BlockSpec((None, d, tf), lambda i, f: (layer, 0, f)),
            pl.BlockSpec((None, d, tf), lambda i, f: (layer, 0, f)),
            pl.BlockSpec((None, tf, d), lambda i, f: (layer, f, 0)),
            pl.BlockSpec((None, 2, d), lambda i, f: (layer, 0, 0)),
        ],
        out_specs=[pl.BlockSpec((tm, d), lambda i, f: (i, 0)), pl.BlockSpec((tm, d), lambda i, f: (i, 0))],
        out_shape=[jax.ShapeDtypeStruct((m, d), F32), jax.ShapeDtypeStruct((m, d), BF16)],
        scratch_shapes=[pltpu.VMEM((tm, d), F32)],
        compiler_params=_cparams("parallel", "arbitrary"),
        name="ffn",
    )(hb, res, wg, wu, wd, ln)


def kernel(x_prompt, x_sample, state_rwkv, state_shift, state_hgrn, p_prompt, p_sample, ln_in_g, ln_in_b, w_in, a_ln_g, a_ln_b, a_ws, a_bs, b_mu, b_w0, b_w_up, b_a0, b_a_up, b_g_up, b_k_k, b_k_a, b_r_k, b_gn_g, b_gn_b, c_lower_bounds, c_norm_g, w_out, ln1_g, ln1_b, w_ffn_gate, w_ffn_up, w_ffn_down, w_ple_gate, w_ple_proj, ln2_g, ln2_b):
    batch, seq, d = x_prompt.shape
    n_s = x_sample.shape[0]
    depth = w_in.shape[0]
    n_p = batch * seq
    wa, wb, wc = a_ln_g.shape[1], b_w0.shape[1], c_norm_g.shape[1]
    pa, pb = 2 * wa, 3 * wb + LORA_W + LORA_A + LORA_G
    pc = w_in.shape[2] - pa - pb
    alpha = float((2 * depth) ** 0.25)
    assert x_sample.shape[1] == 1 and seq % A_CHUNK == 0 and seq % C_CHUNK == 0 and seq % RWKV_CHUNK == 0
    assert n_s % ROW_BLOCK == 0 and n_p % ROW_BLOCK == 0 and (wb // B_HEAD) % 2 == 0 and wb % MXU_TILE == 0

    bf = lambda t: t.astype(BF16)
    w_in_b, w_out_b, w_gate_b, w_up_b, w_down_b = bf(w_in), bf(w_out), bf(w_ffn_gate), bf(w_ffn_up), bf(w_ffn_down)
    w_pg_b, w_pp_b = bf(w_ple_gate), bf(w_ple_proj)
    p_all = bf(jnp.concatenate([p_prompt.reshape(depth, n_p, -1), p_sample.reshape(depth, n_s, -1)], axis=1))
    sgu_ln = jnp.stack([a_ln_g, a_ln_b], axis=1)
    sgu_bst = jnp.swapaxes(a_bs, 1, 2)
    rwkv_vec = jnp.stack([b_w0, b_a0, b_k_k, b_k_a, b_r_k.reshape(depth, wb), b_gn_g, b_gn_b, jnp.zeros_like(b_w0)], axis=1)
    idx = jnp.arange(MXU_TILE)
    ones_bd = bf(idx[:, None] // B_HEAD == idx[None, :] // B_HEAD)
    rwkv_params = (b_mu.reshape(depth, 1, pb), rwkv_vec, bf(b_w_up), bf(b_a_up), bf(b_g_up), ones_bd)
    hgrn_ng = c_norm_g.reshape(depth, 1, wc)
    ln1 = jnp.stack([ln1_g, ln1_b], axis=1)
    ln2 = jnp.stack([ln2_g, ln2_b], axis=1)
    row = lambda t: t.reshape(1, -1)

    h, hb = _ln_in(x_prompt.reshape(n_p, d), x_sample.reshape(n_s, d), row(ln_in_g), row(ln_in_b))

    rwkv_s = jnp.zeros(state_rwkv.shape, F32)
    hgrn_s = jnp.zeros(state_hgrn.shape, F32)
    rwkv_p, shift_p, hgrn_p, shift_s, sgu_v = [], [], [], [], []
    for l in range(depth):
        za = _mm(hb, w_in_b, l, 0, pa, "proj_a")
        zb = _mm(hb, w_in_b, l, pa, pb, "proj_b")
        zc = _mm(hb, w_in_b, l, pa + pb, pc, "proj_c")

        ya, v_rows = _sgu(za, sgu_ln, a_ws, sgu_bst, l, n_p)
        yb_s, rwkv_s = _rwkv_sample(zb, state_shift, state_rwkv, rwkv_s, rwkv_params, l, n_p)
        yb, r_p = _rwkv_prompt(zb, yb_s, rwkv_params, l, batch, seq)
        yc_s, hgrn_s = _hgrn_sample(zc, c_lower_bounds, hgrn_ng, state_hgrn, hgrn_s, l, n_p)
        yc, c_p = _hgrn_prompt(zc, yc_s, c_lower_bounds, hgrn_ng, l, batch, seq)

        hb, res = _out_proj(ya, yb, yc, h, w_out_b, ln1, p_all, w_pg_b, w_pp_b, l, alpha)
        h, hb = _ffn(hb, res, w_gate_b, w_up_b, w_down_b, ln2, l)

        rwkv_p.append(r_p)
        shift_p.append(zb[seq - 1:n_p:seq])
        hgrn_p.append(c_p)
        shift_s.append(zb[n_p:])
        sgu_v.append(v_rows.reshape(n_s, 1, wa))

    return (h[:n_p].reshape(batch, seq, d), h[n_p:].reshape(n_s, 1, d), jnp.stack(rwkv_p), jnp.stack(shift_p),
            jnp.stack(hgrn_p), rwkv_s, jnp.stack(shift_s), hgrn_s, jnp.stack(sgu_v))
```

```python
import functools

import jax
import jax.numpy as jnp
from jax import lax
from jax.experimental import pallas as pl
from jax.experimental.pallas import tpu as pltpu

F32 = jnp.float32
BF16 = jnp.bfloat16

A_GROUPS = 4
A_CHUNK = 128
B_HEAD = 64
LORA_W, LORA_A, LORA_G = 64, 64, 128
C_HEAD = 128
C_CHUNK = 128
LN_EPS = 1e-5
B_GN_EPS = 1e-5 * B_HEAD
RMS_EPS = 1e-6

RWKV_CHUNK = 64
RWKV_STEP_ROWS = 128
HGRN_SUB = 16
SAMPLE_ROWS = 16
ROW_BLOCK = 128
LANES = 128
SUBLANES = 8
MXU_TILE = 256
VMEM_LIMIT_BYTES = 56 * 1024 * 1024


def _cparams(*sem):
    return pltpu.CompilerParams(dimension_semantics=sem, vmem_limit_bytes=VMEM_LIMIT_BYTES)


def _pick_tile(n, target, align):
    best = None
    for t in range(align, min(n, target) + 1, align):
        if n % t == 0:
            best = t
    assert best is not None, (n, target, align)
    return best


def _dot(a, b):
    return jnp.dot(a, b, preferred_element_type=F32)


def _dot_nt(a, b):
    return lax.dot_general(a, b, (((1,), (1,)), ((), ())), preferred_element_type=F32)


def _dot_tn(a, b):
    return lax.dot_general(a, b, (((0,), (0,)), ((), ())), preferred_element_type=F32)


def _bdot(a, b):
    return jnp.dot(a.astype(BF16), b.astype(BF16), preferred_element_type=F32)


def _bdot_nt(a, b):
    return _dot_nt(a.astype(BF16), b.astype(BF16))


def _layer_norm(x, g, b, eps):
    mu = jnp.mean(x, axis=-1, keepdims=True)
    xc = x - mu
    var = jnp.mean(xc * xc, axis=-1, keepdims=True)
    return xc * lax.rsqrt(var + eps) * g + b


def _gelu(x):
    return 0.5 * x * (1.0 + lax.erf(x * 0.7071067811865476))


def _sigmoid(x):
    return 1.0 / (1.0 + jnp.exp(-x))


def _log_sigmoid(x):
    return jnp.minimum(x, 0.0) - jnp.log1p(jnp.exp(-jnp.abs(x)))


def _split_bf16(x):
    hi = x.astype(BF16).astype(F32)
    return hi, x - hi


def _seg_sum(x, ones_bd):
    rows, width = x.shape
    t = ones_bd.shape[0]
    nb = width // t
    hi = x.astype(BF16)
    lo = (x - hi.astype(F32)).astype(BF16)
    parts = [p[:, j * t:(j + 1) * t] for p in (hi, lo) for j in range(nb)]
    r = _dot(jnp.concatenate(parts, axis=0), ones_bd)
    return jnp.concatenate(
        [r[j * rows:(j + 1) * rows] + r[(nb + j) * rows:(nb + j + 1) * rows] for j in range(nb)], axis=1)


def _cumsum_rows(x, period=None):
    rows, width = x.shape
    hi = x.astype(BF16)
    r1 = x - hi.astype(F32)
    mid = r1.astype(BF16)
    lo = (r1 - mid.astype(F32)).astype(BF16)
    ri = lax.broadcasted_iota(jnp.int32, (rows, rows), 0)
    ci = lax.broadcasted_iota(jnp.int32, (rows, rows), 1)
    tri = ri >= ci
    if period is not None and period < rows:
        tri = tri & (ci >= (ri // period) * period)
    c = _dot(jnp.where(tri, 1.0, 0.0).astype(BF16), jnp.concatenate([hi, mid, lo], axis=1))
    return c[:, :width] + c[:, width:2 * width] + c[:, 2 * width:]


def _layer_block(arr, layer):
    nd = arr.ndim - 1
    return pl.BlockSpec((None,) + arr.shape[1:], lambda *_: (layer,) + (0,) * nd)


def _whole(arr):
    nd = arr.ndim
    return pl.BlockSpec(arr.shape, lambda *_: (0,) * nd)


def _ln_in_kernel(xp_ref, xs_ref, g_ref, b_ref, h_ref, hb_ref, *, n_prompt_blocks):
    i = pl.program_id(0)

    def emit(x):
        h = _layer_norm(x, g_ref[...], b_ref[...], LN_EPS)
        h_ref[...] = h
        hb_ref[...] = h.astype(BF16)

    @pl.when(i < n_prompt_blocks)
    def _():
        emit(xp_ref[...])

    @pl.when(i >= n_prompt_blocks)
    def _():
        emit(xs_ref[...])


def _ln_in(xp, xs, g, b):
    n_p, d = xp.shape
    n_s = xs.shape[0]
    npb, nsb = n_p // ROW_BLOCK, n_s // ROW_BLOCK
    m = n_p + n_s
    return pl.pallas_call(
        functools.partial(_ln_in_kernel, n_prompt_blocks=npb),
        grid=(npb + nsb,),
        in_specs=[
            pl.BlockSpec((ROW_BLOCK, d), lambda i: (jnp.minimum(i, npb - 1), 0)),
            pl.BlockSpec((ROW_BLOCK, d), lambda i: (jnp.maximum(i - npb, 0), 0)),
            pl.BlockSpec((1, d), lambda i: (0, 0)),
            pl.BlockSpec((1, d), lambda i: (0, 0)),
        ],
        out_specs=[pl.BlockSpec((ROW_BLOCK, d), lambda i: (i, 0)), pl.BlockSpec((ROW_BLOCK, d), lambda i: (i, 0))],
        out_shape=[jax.ShapeDtypeStruct((m, d), F32), jax.ShapeDtypeStruct((m, d), BF16)],
        compiler_params=_cparams("parallel"),
        name="ln_in",
    )(xp, xs, g, b)


def _mm_kernel(x_ref, w_ref, o_ref):
    o_ref[...] = jnp.dot(x_ref[...], w_ref[...], preferred_element_type=F32)


def _mm(xb, w, layer, name):
    m, k = xb.shape
    n = w.shape[2]
    tm = _pick_tile(m, 640, 16)
    tn = _pick_tile(n, 1536, 128)
    return pl.pallas_call(
        _mm_kernel,
        grid=(m // tm, n // tn),
        in_specs=[pl.BlockSpec((tm, k), lambda i, j: (i, 0)),
                  pl.BlockSpec((None, k, tn), lambda i, j: (layer, 0, j))],
        out_specs=pl.BlockSpec((tm, tn), lambda i, j: (i, j)),
        out_shape=jax.ShapeDtypeStruct((m, n), F32),
        compiler_params=_cparams("parallel", "parallel"),
        name=name,
    )(xb, w)


def _sgu_kernel(u_ref, v_ref, ln_ref, ws_ref, bst_ref, y_ref, vn_ref, *, n_prompt_blocks):
    i = pl.program_id(0)
    gd = u_ref.shape[1] // A_GROUPS
    u = _gelu(u_ref[...])
    v = _gelu(v_ref[...])
    vn = [
        _layer_norm(v[:, g * gd:(g + 1) * gd], ln_ref[0:1, g * gd:(g + 1) * gd], ln_ref[1:2, g * gd:(g + 1) * gd], LN_EPS)
        for g in range(A_GROUPS)
    ]

    @pl.when(i < n_prompt_blocks)
    def _():
        row = lax.broadcasted_iota(jnp.int32, (A_CHUNK, A_CHUNK), 0)
        col = lax.broadcasted_iota(jnp.int32, (A_CHUNK, A_CHUNK), 1)
        for g in range(A_GROUPS):
            w_causal = jnp.where(row >= col, ws_ref[g], 0.0)
            mixed = _bdot(w_causal, vn[g]) + bst_ref[:, g:g + 1]
            y_ref[:, g * gd:(g + 1) * gd] = (u[:, g * gd:(g + 1) * gd] * mixed).astype(BF16)

    @pl.when(i >= n_prompt_blocks)
    def _():
        for g in range(A_GROUPS):
            mixed = vn[g] * ws_ref[g, 0:1, 0:1] + bst_ref[0:1, g:g + 1]
            y_ref[:, g * gd:(g + 1) * gd] = (u[:, g * gd:(g + 1) * gd] * mixed).astype(BF16)
            vn_ref[:, g * gd:(g + 1) * gd] = vn[g]


def _sgu(za, ln, ws, bst, layer, n_prompt):
    m = za.shape[0]
    wa = za.shape[1] // 2
    npb = n_prompt // A_CHUNK
    nb = m // A_CHUNK
    n_s = m - n_prompt
    return pl.pallas_call(
        functools.partial(_sgu_kernel, n_prompt_blocks=npb),
        grid=(nb,),
        in_specs=[
            pl.BlockSpec((A_CHUNK, wa), lambda i: (i, 0)),
            pl.BlockSpec((A_CHUNK, wa), lambda i: (i, 1)),
            _layer_block(ln, layer), _layer_block(ws, layer), _layer_block(bst, layer),
        ],
        out_specs=[
            pl.BlockSpec((A_CHUNK, wa), lambda i: (i, 0)),
            pl.BlockSpec((A_CHUNK, wa), lambda i: (jnp.maximum(i - npb, 0), 0)),
        ],
        out_shape=[jax.ShapeDtypeStruct((m, wa), BF16), jax.ShapeDtypeStruct((n_s, wa), F32)],
        compiler_params=_cparams("arbitrary"),
        name="sgu",
    )(za, za, ln, ws, bst)


_V_W0, _V_A0, _V_KK, _V_KA, _V_RK, _V_GNG, _V_GNB = range(7)


def _rwkv_prep(zb, prev, mu, vec, wup, aup, gup, ones_bd):
    wb = vec.shape[1]
    row = lambda j: vec[j:j + 1, :]
    xs = zb + mu * (prev - zb)
    r, k, v = xs[:, :wb], xs[:, wb:2 * wb], xs[:, 2 * wb:3 * wb]
    o4, o5 = 3 * wb + LORA_W, 3 * wb + LORA_W + LORA_A
    wd, ad, gd = xs[:, 3 * wb:o4], xs[:, o4:o5], xs[:, o5:]
    w = row(_V_W0) + _bdot(jnp.tanh(wd), wup)
    softplus_neg_w = jnp.maximum(-w, 0.0) + jnp.log1p(jnp.exp(-jnp.abs(w)))
    log_decay = -jnp.exp(-softplus_neg_w - 0.5)
    a = _sigmoid(row(_V_A0) + _bdot(ad, aup))
    g = _bdot(_sigmoid(gd), gup)
    kk = k * row(_V_KK)
    kk = kk / jnp.maximum(jnp.sqrt(_seg_sum(kk * kk, ones_bd)), 1e-12)
    kd = k * (1.0 + (a - 1.0) * row(_V_KA))
    bonus = _seg_sum(r * kd * row(_V_RK), ones_bd) * v
    return r, log_decay, kd, v, kk, a, g, bonus


def _rwkv_finish(y, bonus, g, vec, ones_bd):
    mean = _seg_sum(y, ones_bd) * (1.0 / B_HEAD)
    yc = y - mean
    var = _seg_sum(yc * yc, ones_bd) * (1.0 / B_HEAD)
    return (yc * lax.rsqrt(var + B_GN_EPS) * vec[_V_GNG:_V_GNG + 1, :] + vec[_V_GNB:_V_GNB + 1, :] + bonus) * g


def _rwkv_prompt_kernel(zb_ref, ys_ref, mu_ref, vec_ref, wup_ref, aup_ref, gup_ref, ones_ref,
                        y_ref, sout_ref, shift_ref, state_scr, prev_scr, *, steps_per_seq, n_prompt_steps):
    i = pl.program_id(0)

    @pl.when(i >= n_prompt_steps)
    def _():
        y_ref[...] = ys_ref[...]

    @pl.when(i < n_prompt_steps)
    def _():
        s_idx = lax.rem(i, steps_per_seq)

        @pl.when(s_idx == 0)
        def _():
            state_scr[...] = jnp.zeros_like(state_scr)
            prev_scr[...] = jnp.zeros_like(prev_scr)

        zb = zb_ref[...]
        R = zb.shape[0]
        C = RWKV_CHUNK
        n_pairs = state_scr.shape[0]
        N = B_HEAD
        P = 2 * N
        ones_bd = ones_ref[...]
        vec = vec_ref[...]
        row1 = lax.broadcasted_iota(jnp.int32, (R, 1), 0)
        prev = jnp.where(row1 == 0, prev_scr[...], pltpu.roll(zb, 1, 0))
        prev_scr[...] = zb[R - 1:R, :]
        r_all, lw_all, kd_all, v_all, kk_all, a_all, g, bonus = _rwkv_prep(
            zb, prev, mu_ref[...], vec, wup_ref[...], aup_ref[...], gup_ref[...], ones_bd)
        cum_all = _cumsum_rows(lw_all, C)

        lane = lax.broadcasted_iota(jnp.int32, (C, P), 1)
        rowc = lax.broadcasted_iota(jnp.int32, (C, P), 0)
        lo = lane < N
        col_in = jnp.where(lo, lane, lane - N)
        incl2 = rowc >= col_in
        strict2 = rowc > col_in
        lo2 = lax.broadcasted_iota(jnp.int32, (2 * C, P), 1) < N
        bd_mask = (lax.broadcasted_iota(jnp.int32, (P, P), 0) < N) == (lax.broadcasted_iota(jnp.int32, (P, P), 1) < N)
        pairs = [slice(p * P, (p + 1) * P) for p in range(n_pairs)]

        def local_part(c):
            rs = slice(c * C, (c + 1) * C)
            r, lw, kd, v, kk, a, cum = (t[rs] for t in (r_all, lw_all, kd_all, v_all, kk_all, a_all, cum_all))
            m = cum[C // 2 - 1:C // 2, :]
            cend = cum[C - 1:C, :]
            e_pos = jnp.exp(cum - m)
            e_neg = jnp.exp(m - cum)
            e_prev = jnp.exp(cum - lw - m)
            e_m = jnp.exp(m)
            e_end = jnp.exp(cend - m)
            rp = r * e_pos
            kkp = kk * e_prev
            kp = kd * e_neg
            bp = kk * a * e_neg
            rhat = rp * e_m
            kkhat = kkp * e_m
            ktil = kp * e_end
            btil = bp * e_end
            e_cend = e_end * e_m
            ar1, ar2, n_bd, xs, v_b = [], [], [], [], []
            for ps in pairs:
                q_pair = jnp.concatenate([rp[:, ps], kkp[:, ps]], axis=0)
                q1 = jnp.where(lo2, q_pair, 0.0).astype(BF16)
                q2 = jnp.where(lo2, 0.0, q_pair).astype(BF16)
                kp_b, bp_b = kp[:, ps].astype(BF16), bp[:, ps].astype(BF16)
                sc1 = _dot_nt(q1, jnp.concatenate([bp_b, kp_b], axis=0))
                sc2 = _dot_nt(q2, jnp.concatenate([kp_b, bp_b], axis=0))
                ar1.append(jnp.where(incl2, sc1[:C], 0.0).astype(BF16))
                ar2.append(jnp.where(incl2, sc2[:C], 0.0).astype(BF16))
                kn1 = jnp.where(strict2, sc1[C:], 0.0)
                kn2 = jnp.where(strict2, sc2[C:], 0.0)
                n_bd.append(jnp.concatenate([jnp.where(lo, kn1, 0.0), jnp.where(lo, 0.0, kn2)], axis=0).astype(BF16))
                akk = jnp.concatenate([jnp.where(lo, 0.0, kn1), jnp.where(lo, kn2, 0.0)], axis=0).astype(BF16)
                v_pair = v[:, ps]
                v_b.append(v_pair.astype(BF16))
                v_sw = pltpu.roll(v_pair, N, 1).astype(BF16)
                av = _dot(akk, jnp.concatenate([v_sw, v_sw], axis=0))
                kkh = kkhat[:, ps]
                xs.append(jnp.concatenate([jnp.where(lo, kkh, av[:C]), jnp.where(lo, av[C:], kkh)], axis=0))
            xs = [x - _dot(nb, x.astype(BF16)) for nb, x in zip(n_bd, xs)]
            pw = n_bd
            span = 2
            while span < C:
                pw = [_dot(t, t).astype(BF16) for t in pw]
                xs = [x + _dot(t, x.astype(BF16)) for t, x in zip(pw, xs)]
                span *= 2
            return dict(xs=xs, ar1=ar1, ar2=ar2, v_b=v_b, rhat=rhat, ktil=ktil, btil=btil, e_cend=e_cend)

        def state_part(loc, states):
            ys, new_states = [], []
            for p, ps in enumerate(pairs):
                x = loc["xs"][p]
                w1 = jnp.where(lo, x[:C], x[C:])
                w2 = pltpu.roll(jnp.where(lo, x[C:], x[:C]), N, 1)
                s_bd = states[p]
                qs = _dot_nt(jnp.concatenate([w1, loc["rhat"][:, ps]], axis=0).astype(BF16), s_bd.astype(BF16))
                u = -(qs[:C] + w2)
                u_hi, u_lo = _split_bf16(u)
                u_b = u_hi.astype(BF16)
                v_b = loc["v_b"][p]
                y1 = _dot(loc["ar1"][p], jnp.concatenate([u_b, v_b], axis=0))
                y2 = _dot(loc["ar2"][p], jnp.concatenate([v_b, u_b], axis=0))
                ys.append(qs[C:] + jnp.where(lo, y1, y2))
                kb = jnp.concatenate([loc["ktil"][:, ps], loc["btil"][:, ps], loc["btil"][:, ps]], axis=0).astype(BF16)
                vu = jnp.concatenate([v_b, u_b, u_lo.astype(BF16)], axis=0)
                new_states.append(jnp.where(bd_mask, s_bd * loc["e_cend"][:, ps] + _dot_tn(vu, kb), 0.0))
            return jnp.concatenate(ys, axis=1), new_states

        locs = [local_part(c) for c in range(R // C)]
        states = [state_scr[p] for p in range(n_pairs)]
        y_rows = []
        for loc in locs:
            y_c, states = state_part(loc, states)
            y_rows.append(y_c)
        for p in range(n_pairs):
            state_scr[p] = states[p]

        y = jnp.concatenate(y_rows, axis=0)
        y_ref[...] = _rwkv_finish(y, bonus, g, vec, ones_bd).astype(BF16)

        @pl.when(s_idx == steps_per_seq - 1)
        def _():
            shift_ref[0] = zb[R - 1:R, :]
            for p in range(n_pairs):
                sout_ref[0, 2 * p] = states[p][:N, :N]
                sout_ref[0, 2 * p + 1] = states[p][N:, N:]


def _rwkv_prompt(zb, y_sample, params, layer, batch, seq):
    mu, vec, wup, aup, gup, ones_bd = params
    pb = zb.shape[1]
    wb = vec.shape[2]
    n_heads = wb // B_HEAD
    R = RWKV_STEP_ROWS
    sps = seq // R
    n_steps = batch * sps
    n_s = y_sample.shape[0]
    m = batch * seq + n_s
    seq_of = lambda i: jnp.minimum(i // sps, batch - 1)
    return pl.pallas_call(
        functools.partial(_rwkv_prompt_kernel, steps_per_seq=sps, n_prompt_steps=n_steps),
        grid=(n_steps + n_s // R,),
        in_specs=[
            pl.BlockSpec((R, pb), lambda i: (jnp.minimum(i, n_steps - 1), 0)),
            pl.BlockSpec((R, wb), lambda i: (jnp.maximum(i - n_steps, 0), 0)),
            _layer_block(mu, layer), _layer_block(vec, layer), _layer_block(wup, layer),
            _layer_block(aup, layer), _layer_block(gup, layer), _whole(ones_bd),
        ],
        out_specs=[
            pl.BlockSpec((R, wb), lambda i: (i, 0)),
            pl.BlockSpec((1, n_heads, B_HEAD, B_HEAD), lambda i: (seq_of(i), 0, 0, 0)),
            pl.BlockSpec((1, 1, pb), lambda i: (seq_of(i), 0, 0)),
        ],
        out_shape=[
            jax.ShapeDtypeStruct((m, wb), BF16),
            jax.ShapeDtypeStruct((batch, n_heads, B_HEAD, B_HEAD), F32),
            jax.ShapeDtypeStruct((batch, 1, pb), F32),
        ],
        scratch_shapes=[pltpu.VMEM((n_heads // 2, 2 * B_HEAD, 2 * B_HEAD), F32), pltpu.VMEM((1, pb), F32)],
        compiler_params=_cparams("arbitrary"),
        name="rwkv_prompt",
    )(zb, y_sample, mu, vec, wup, aup, gup, ones_bd)


def _rwkv_sample_kernel(zb_ref, shift_ref, mu_ref, vec_ref, wup_ref, aup_ref, gup_ref, ones_ref, s_ref, buf_ref,
                        y_ref, sout_ref, shift_out_ref,
                        kk_t, wr_t, b_t, kd_t, v_t, w_t, r_t, y_t, g_scr, bonus_scr):
    del buf_ref
    p = pl.program_id(1)
    heads_per_step = s_ref.shape[0]
    N = B_HEAD

    @pl.when(p == 0)
    def _():
        zb = zb_ref[...]
        shift_out_ref[...] = zb
        r, lw, kd, v, kk, a, g, bonus = _rwkv_prep(
            zb, shift_ref[...], mu_ref[...], vec_ref[...], wup_ref[...], aup_ref[...], gup_ref[...], ones_ref[...])
        w = jnp.exp(lw)
        kk_t[...] = kk.T
        wr_t[...] = (w * r).T
        b_t[...] = (kk * a).T
        kd_t[...] = kd.T
        v_t[...] = v.T
        w_t[...] = w.T
        r_t[...] = r.T
        g_scr[...] = g
        bonus_scr[...] = bonus

    for hh in range(heads_per_step):
        base = pl.multiple_of((p * heads_per_step + hh) * N, N)
        hs = pl.ds(base, N)
        kk_h, wr_h, b_h, kd_h, w_h, r_h = kk_t[hs, :], wr_t[hs, :], b_t[hs, :], kd_t[hs, :], w_t[hs, :], r_t[hs, :]
        b_dot_r = jnp.sum(b_h * r_h, axis=0, keepdims=True)
        k_dot_r = jnp.sum(kd_h * r_h, axis=0, keepdims=True)

        def body(vi, carry):
            s0 = s_ref[hh, vi]
            s_kk = jnp.sum(s0 * kk_h, axis=0, keepdims=True)
            yq = jnp.sum(s0 * wr_h, axis=0, keepdims=True)
            v_row = v_t[pl.ds(base + vi, 1), :]
            sout_ref[hh, vi] = s0 * w_h - s_kk * b_h + v_row * kd_h
            y_t[pl.ds(base + vi, 1), :] = yq - s_kk * b_dot_r + v_row * k_dot_r
            return carry

        lax.fori_loop(0, N, body, 0, unroll=4)

    @pl.when(p == pl.num_programs(1) - 1)
    def _():
        y = y_t[...].T
        y_ref[...] = _rwkv_finish(y, bonus_scr[...], g_scr[...], vec_ref[...], ones_ref[...]).astype(BF16)


def _rwkv_sample(zb, shift, state_t, out_buf, params, layer, n_prompt):
    mu, vec, wup, aup, gup, ones_bd = params
    pb = zb.shape[1]
    wb = vec.shape[2]
    n_heads, n_s = state_t.shape[1], state_t.shape[4]
    hps = 2
    off = n_prompt // LANES
    sspec = pl.BlockSpec((None, hps, B_HEAD, B_HEAD, LANES), lambda sb, p: (layer, p, 0, 0, sb))
    return pl.pallas_call(
        _rwkv_sample_kernel,
        grid=(n_s // LANES, n_heads // hps),
        in_specs=[
            pl.BlockSpec((LANES, pb), lambda sb, p: (off + sb, 0)),
            pl.BlockSpec((None, LANES, pb), lambda sb, p: (layer, sb, 0)),
            _layer_block(mu, layer), _layer_block(vec, layer), _layer_block(wup, layer),
            _layer_block(aup, layer), _layer_block(gup, layer), _whole(ones_bd),
            sspec, pl.BlockSpec(memory_space=pl.ANY),
        ],
        out_specs=[pl.BlockSpec((LANES, wb), lambda sb, p: (sb, 0)), sspec,
                   pl.BlockSpec((LANES, pb), lambda sb, p: (sb, 0))],
        out_shape=[jax.ShapeDtypeStruct((n_s, wb), BF16), jax.ShapeDtypeStruct(state_t.shape, F32),
                   jax.ShapeDtypeStruct((n_s, pb), F32)],
        input_output_aliases={9: 1},
        scratch_shapes=[pltpu.VMEM((wb, LANES), F32)] * 8 + [pltpu.VMEM((LANES, wb), F32)] * 2,
        compiler_params=_cparams("arbitrary", "arbitrary"),
        name="rwkv_sample",
    )(zb, shift, mu, vec, wup, aup, gup, ones_bd, state_t, out_buf)


def _hgrn_gates(fz, clb, layer):
    ls_pos = _log_sigmoid(fz)
    if layer == 0:
        return ls_pos, _sigmoid(-fz)
    e = jnp.exp(clb - jnp.max(clb, axis=0, keepdims=True))
    sm = e / jnp.sum(e, axis=0, keepdims=True)
    lb = jnp.sum(sm[1:layer + 1], axis=0, keepdims=True)
    x1 = ls_pos
    x2 = jnp.log(lb) + _log_sigmoid(-fz)
    log_f = jnp.maximum(x1, x2) + jnp.log1p(jnp.exp(-jnp.abs(x1 - x2)))
    return log_f, (1.0 - lb) * _sigmoid(-fz)


def _hgrn_finish(o, g, norm_g):
    on = o * lax.rsqrt(jnp.mean(o * o, axis=-1, keepdims=True) + RMS_EPS) * norm_g
    return on * (g * _sigmoid(g))


def _hgrn_prompt_kernel(zc_ref, ys_ref, clb_ref, ng_ref, y_ref, sout_ref, state_scr, *, layer, n_chunks, n_prompt_steps):
    i = pl.program_id(0)

    @pl.when(i >= n_prompt_steps)
    def _():
        y_ref[...] = ys_ref[...]

    @pl.when(i < n_prompt_steps)
    def _():
        c_idx = lax.rem(i, n_chunks)

        @pl.when(c_idx == 0)
        def _():
            state_scr[...] = jnp.zeros_like(state_scr)

        z = zc_ref[...]
        C = z.shape[0]
        wc = z.shape[1] // 4
        n_heads = wc // C_HEAD
        D = C_HEAD
        SUB = HGRN_SUB
        HALF = SUBLANES
        nsub = C // SUB
        q, fz, iv, g = z[:, :wc], z[:, wc:2 * wc], z[:, 2 * wc:3 * wc], z[:, 3 * wc:]
        log_f, kg = _hgrn_gates(fz, clb_ref[...], layer)
        bcum = _cumsum_rows(log_f)
        e_b = jnp.exp(bcum)
        b_end = bcum[C - 1:C, :]
        e_end = jnp.exp(b_end)
        k_hat = kg * jnp.exp(b_end - bcum)
        t_full = lax.broadcasted_iota(jnp.int32, (nsub, SUB, 1), 1)
        t_half = lax.broadcasted_iota(jnp.int32, (nsub, SUB - HALF, 1), 1) + HALF
        level_sizes = [SUB << li for li in range((C // SUB).bit_length() - 1)]
        half = C // 2
        n_lv = len(level_sizes) * half
        ri = lax.broadcasted_iota(jnp.int32, (n_lv, n_lv), 0)
        ci = lax.broadcasted_iota(jnp.int32, (n_lv, n_lv), 1)
        level_mask = None
        for li, s in enumerate(level_sizes):
            shift = s.bit_length() - 1
            in_level = (ri >= li * half) & (ri < (li + 1) * half) & (ci >= li * half) & (ci < (li + 1) * half)
            same_block = ((ri - li * half) >> shift) == ((ci - li * half) >> shift)
            lm = in_level & same_block
            level_mask = lm if level_mask is None else (level_mask | lm)

        outs = []
        for h in range(n_heads):
            sl = slice(h * D, (h + 1) * D)
            qh, kh, vh, bh = q[:, sl], kg[:, sl], iv[:, sl], bcum[:, sl]
            st = state_scr[h]
            o = _bdot_nt(qh * e_b[:, sl], st)
            q3, k3, v3, b3 = (t.reshape(nsub, SUB, D) for t in (qh, kh, vh, bh))
            q3h, b3h = q3[:, HALF:, :], b3[:, HALF:, :]
            od = jnp.zeros((nsub, SUB, D), F32)
            odh = jnp.zeros((nsub, SUB - HALF, D), F32)
            for j in range(SUB):
                if j < HALF:
                    dec = jnp.exp(b3 - b3[:, j:j + 1, :])
                    att = jnp.sum(q3 * k3[:, j:j + 1, :] * dec, axis=-1, keepdims=True)
                    od = od + jnp.where(t_full >= j, att, 0.0) * v3[:, j:j + 1, :]
                else:
                    dec = jnp.exp(b3h - b3[:, j:j + 1, :])
                    att = jnp.sum(q3h * k3[:, j:j + 1, :] * dec, axis=-1, keepdims=True)
                    odh = odh + jnp.where(t_half >= j, att, 0.0) * v3[:, j:j + 1, :]
            od = od + jnp.concatenate([jnp.zeros((nsub, HALF, D), F32), odh], axis=1)
            o = o + od.reshape(C, D)
            q_l, k_l, v_l = [], [], []
            for s in level_sizes:
                for j in range(C // (2 * s)):
                    a0 = 2 * j * s
                    b_bound = bh[a0 + s - 1:a0 + s, :]
                    q_l.append(qh[a0 + s:a0 + 2 * s, :] * jnp.exp(bh[a0 + s:a0 + 2 * s, :] - b_bound))
                    k_l.append(kh[a0:a0 + s, :] * jnp.exp(b_bound - bh[a0:a0 + s, :]))
                    v_l.append(vh[a0:a0 + s, :])
            att = _bdot_nt(jnp.concatenate(q_l, axis=0), jnp.concatenate(k_l, axis=0))
            o_lv = _bdot(jnp.where(level_mask, att, 0.0), jnp.concatenate(v_l, axis=0))
            for li, s in enumerate(level_sizes):
                pieces = []
                for j in range(C // (2 * s)):
                    r0 = li * (C // 2) + j * s
                    pieces += [jnp.zeros((s, D), F32), o_lv[r0:r0 + s, :]]
                o = o + jnp.concatenate(pieces, axis=0)
            outs.append(o)
            vh_hi, vh_lo = _split_bf16(vh)
            kh_hi, kh_lo = _split_bf16(k_hat[:, sl])
            upd = _dot_tn(jnp.concatenate([vh_hi, vh_hi, vh_lo], axis=0).astype(BF16),
                          jnp.concatenate([kh_hi, kh_lo, kh_hi], axis=0).astype(BF16))
            state_scr[h] = st * e_end[:, sl] + upd

        o_all = jnp.concatenate(outs, axis=1)
        y_ref[...] = _hgrn_finish(o_all, g, ng_ref[...]).astype(BF16)

        @pl.when(c_idx == n_chunks - 1)
        def _():
            for h in range(n_heads):
                sout_ref[0, h] = state_scr[h].T


def _hgrn_prompt(zc, y_sample, clb, norm_g, layer, batch, seq):
    pc = zc.shape[1]
    wc = pc // 4
    n_heads = wc // C_HEAD
    C = C_CHUNK
    nc = seq // C
    n_steps = batch * nc
    n_s = y_sample.shape[0]
    m = batch * seq + n_s
    return pl.pallas_call(
        functools.partial(_hgrn_prompt_kernel, layer=layer, n_chunks=nc, n_prompt_steps=n_steps),
        grid=(n_steps + n_s // C,),
        in_specs=[
            pl.BlockSpec((C, pc), lambda i: (jnp.minimum(i, n_steps - 1), 0)),
            pl.BlockSpec((C, wc), lambda i: (jnp.maximum(i - n_steps, 0), 0)),
            _whole(clb), _layer_block(norm_g, layer),
        ],
        out_specs=[
            pl.BlockSpec((C, wc), lambda i: (i, 0)),
            pl.BlockSpec((1, n_heads, C_HEAD, C_HEAD), lambda i: (jnp.minimum(i // nc, batch - 1), 0, 0, 0)),
        ],
        out_shape=[
            jax.ShapeDtypeStruct((m, wc), BF16),
            jax.ShapeDtypeStruct((batch, n_heads, C_HEAD, C_HEAD), F32),
        ],
        scratch_shapes=[pltpu.VMEM((n_heads, C_HEAD, C_HEAD), F32)],
        compiler_params=_cparams("arbitrary"),
        name="hgrn_prompt",
    )(zc, y_sample, clb, norm_g)


def _hgrn_sample_kernel(zc_ref, clb_ref, ng_ref, s_ref, buf_ref, y_ref, sout_ref,
                        q_scr, f_scr, k_scr, v_scr, o_scr, *, layer):
    del buf_ref
    z = zc_ref[...]
    n_rows, n_heads = s_ref.shape[0], s_ref.shape[1]
    wc = z.shape[1] // 4
    D = C_HEAD
    q, fz, iv, g = z[:, :wc], z[:, wc:2 * wc], z[:, 2 * wc:3 * wc], z[:, 3 * wc:]
    log_f, kg = _hgrn_gates(fz, clb_ref[...], layer)
    q_scr[...] = q
    f_scr[...] = jnp.exp(log_f)
    k_scr[...] = kg
    v_scr[...] = iv
    pad = jnp.zeros((SUBLANES - n_heads, D), F32)

    def body(i, carry):
        q_r, f_r, k_r, v_r = (t[pl.ds(i, 1), :] for t in (q_scr, f_scr, k_scr, v_scr))
        heads = lambda t: jnp.concatenate([t[:, h * D:(h + 1) * D] for h in range(n_heads)] + [pad], axis=0)
        f_cols = heads(f_r).T
        k_cols = heads(k_r).T
        os_ = []
        for h in range(n_heads):
            sl = slice(h * D, (h + 1) * D)
            s0 = s_ref[i, h]
            qk = jnp.sum(q_r[:, sl] * k_r[:, sl], axis=-1, keepdims=True)
            os_.append(_bdot(q_r[:, sl] * f_r[:, sl], s0) + qk * v_r[:, sl])
            sout_ref[i, h] = s0 * f_cols[:, h:h + 1] + k_cols[:, h:h + 1] * v_r[:, sl]
        o_scr[pl.ds(i, 1), :] = jnp.concatenate(os_, axis=1)
        return carry

    lax.fori_loop(0, n_rows, body, 0)
    y_ref[...] = _hgrn_finish(o_scr[...], g, ng_ref[...]).astype(BF16)


def _hgrn_sample(zc, clb, norm_g, state, out_buf, layer, n_prompt):
    pc = zc.shape[1]
    wc = pc // 4
    n_s, n_heads = state.shape[1], state.shape[2]
    R = SAMPLE_ROWS
    off = n_prompt // R
    sspec = pl.BlockSpec((None, R, n_heads, C_HEAD, C_HEAD), lambda i: (layer, i, 0, 0, 0))
    return pl.pallas_call(
        functools.partial(_hgrn_sample_kernel, layer=layer),
        grid=(n_s // R,),
        in_specs=[pl.BlockSpec((R, pc), lambda i: (off + i, 0)), _whole(clb), _layer_block(norm_g, layer),
                  sspec, pl.BlockSpec(memory_space=pl.ANY)],
        out_specs=[pl.BlockSpec((R, wc), lambda i: (i, 0)), sspec],
        out_shape=[jax.ShapeDtypeStruct((n_s, wc), BF16), jax.ShapeDtypeStruct(state.shape, F32)],
        input_output_aliases={4: 1},
        scratch_shapes=[pltpu.VMEM((R, wc), F32)] * 5,
        compiler_params=_cparams("parallel"),
        name="hgrn_sample",
    )(zc, clb, norm_g, state, out_buf)


def _out_kernel(ya_ref, yb_ref, yc_ref, h_ref, wo_ref, ln_ref, p_ref, wpg_ref, wpp_ref, hb_ref, res_ref, *, alpha):
    wa, wb = ya_ref.shape[1], yb_ref.shape[1]
    mix = (_dot(ya_ref[...], wo_ref[:wa, :]) + _dot(yb_ref[...], wo_ref[wa:wa + wb, :])
           + _dot(yc_ref[...], wo_ref[wa + wb:, :]))
    h1 = _layer_norm(alpha * h_ref[...] + mix, ln_ref[0:1, :], ln_ref[1:2, :], LN_EPS)
    h1b = h1.astype(BF16)
    ple = _sigmoid(_dot(h1b, wpg_ref[...])) * _dot(p_ref[...], wpp_ref[...])
    hb_ref[...] = h1b
    res_ref[...] = alpha * h1 + ple


def _out_proj(ya, yb, yc, h, wo, ln, p, wpg, wpp, layer, alpha):
    m, d = h.shape
    tm = _pick_tile(m, 320, 16)
    rows = lambda a: pl.BlockSpec((tm, a.shape[-1]), lambda i: (i, 0))
    const = lambda a: pl.BlockSpec((None,) + a.shape[1:], lambda i: (layer, 0, 0), pipeline_mode=pl.Buffered(1))
    return pl.pallas_call(
        functools.partial(_out_kernel, alpha=alpha),
        grid=(m // tm,),
        in_specs=[rows(ya), rows(yb), rows(yc), rows(h), const(wo), const(ln),
                  pl.BlockSpec((None, tm, p.shape[-1]), lambda i: (layer, i, 0)), const(wpg), const(wpp)],
        out_specs=[rows(h), rows(h)],
        out_shape=[jax.ShapeDtypeStruct((m, d), BF16), jax.ShapeDtypeStruct((m, d), F32)],
        compiler_params=_cparams("parallel"),
        name="out_proj",
    )(ya, yb, yc, h, wo, ln, p, wpg, wpp)


def _ffn_kernel(hb_ref, res_ref, wg_ref, wu_ref, wd_ref, ln_ref, h_ref, hbo_ref, acc_ref):
    f = pl.program_id(1)

    @pl.when(f == 0)
    def _():
        acc_ref[...] = jnp.zeros_like(acc_ref)

    x = hb_ref[...]
    gate = _dot(x, wg_ref[...])
    up = _dot(x, wu_ref[...])
    act = (gate * _sigmoid(gate) * up).astype(BF16)
    acc_ref[...] += _dot(act, wd_ref[...])

    @pl.when(f == pl.num_programs(1) - 1)
    def _():
        h2 = _layer_norm(res_ref[...] + acc_ref[...], ln_ref[0:1, :], ln_ref[1:2, :], LN_EPS)
        h_ref[...] = h2
        hbo_ref[...] = h2.astype(BF16)


def _ffn(hb, res, wg, wu, wd, ln, layer):
    m, d = hb.shape
    dff = wg.shape[2]
    tm = _pick_tile(m, 640, 16)
    tf = _pick_tile(dff, 512, 128)
    return pl.pallas_call(
        _ffn_kernel,
        grid=(m // tm, dff // tf),
        in_specs=[
            pl.BlockSpec((tm, d), lambda i, f: (i, 0)),
            pl.BlockSpec((tm, d), lambda i, f: (i, 0)),
            pl.BlockSpec((None, d, tf), lambda i, f: (layer, 0, f)),
            pl.BlockSpec((None, d, tf), lambda i, f: (layer, 0, f)),
            pl.BlockSpec((None, tf, d), lambda i, f: (layer, f, 0)),
            pl.BlockSpec((None, 2, d), lambda i, f: (layer, 0, 0)),
        ],
        out_specs=[pl.BlockSpec((tm, d), lambda i, f: (i, 0)), pl.BlockSpec((tm, d), lambda i, f: (i, 0))],
        out_shape=[jax.ShapeDtypeStruct((m, d), F32), jax.ShapeDtypeStruct((m, d), BF16)],
        scratch_shapes=[pltpu.VMEM((tm, d), F32)],
        compiler_params=_cparams("parallel", "arbitrary"),
        name="ffn",
    )(hb, res, wg, wu, wd, ln)


def kernel(x_prompt, x_sample, state_rwkv, state_shift, state_hgrn, p_prompt, p_sample, ln_in_g, ln_in_b, w_in, a_ln_g, a_ln_b, a_ws, a_bs, b_mu, b_w0, b_w_up, b_a0, b_a_up, b_g_up, b_k_k, b_k_a, b_r_k, b_gn_g, b_gn_b, c_lower_bounds, c_norm_g, w_out, ln1_g, ln1_b, w_ffn_gate, w_ffn_up, w_ffn_down, w_ple_gate, w_ple_proj, ln2_g, ln2_b):
    batch, seq, d = x_prompt.shape
    n_s = x_sample.shape[0]
    depth = w_in.shape[0]
    n_p = batch * seq
    wa, wb, wc = a_ln_g.shape[1], b_w0.shape[1], c_norm_g.shape[1]
    pa, pb = 2 * wa, 3 * wb + LORA_W + LORA_A + LORA_G
    alpha = float((2 * depth) ** 0.25)
    assert x_sample.shape[1] == 1 and seq % A_CHUNK == 0 and seq % C_CHUNK == 0 and seq % RWKV_STEP_ROWS == 0
    assert n_s % ROW_BLOCK == 0 and n_p % ROW_BLOCK == 0 and n_s % LANES == 0
    assert (wb // B_HEAD) % 2 == 0 and wb % MXU_TILE == 0

    bf = lambda t: t.astype(BF16)
    w_a_b, w_b_b, w_c_b = bf(w_in[:, :, :pa]), bf(w_in[:, :, pa:pa + pb]), bf(w_in[:, :, pa + pb:])
    w_out_b, w_gate_b, w_up_b, w_down_b = bf(w_out), bf(w_ffn_gate), bf(w_ffn_up), bf(w_ffn_down)
    w_pg_b, w_pp_b = bf(w_ple_gate), bf(w_ple_proj)
    p_all = bf(jnp.concatenate([p_prompt.reshape(depth, n_p, -1), p_sample.reshape(depth, n_s, -1)], axis=1))
    sgu_ln = jnp.stack([a_ln_g, a_ln_b], axis=1)
    sgu_bst = jnp.swapaxes(a_bs, 1, 2)
    rwkv_vec = jnp.stack([b_w0, b_a0, b_k_k, b_k_a, b_r_k.reshape(depth, wb), b_gn_g, b_gn_b, jnp.zeros_like(b_w0)], axis=1)
    idx = jnp.arange(MXU_TILE)
    ones_bd = bf(idx[:, None] // B_HEAD == idx[None, :] // B_HEAD)
    rwkv_params = (b_mu.reshape(depth, 1, pb), rwkv_vec, bf(b_w_up), bf(b_a_up), bf(b_g_up), ones_bd)
    hgrn_ng = c_norm_g.reshape(depth, 1, wc)
    ln1 = jnp.stack([ln1_g, ln1_b], axis=1)
    ln2 = jnp.stack([ln2_g, ln2_b], axis=1)
    row = lambda t: t.reshape(1, -1)

    h, hb = _ln_in(x_prompt.reshape(n_p, d), x_sample.reshape(n_s, d), row(ln_in_g), row(ln_in_b))

    state_rwkv_t = jnp.transpose(state_rwkv, (0, 2, 3, 4, 1))
    rwkv_s = jnp.zeros(state_rwkv_t.shape, F32)
    hgrn_s = jnp.zeros(state_hgrn.shape, F32)
    rwkv_p, shift_p, hgrn_p, shift_s, sgu_v = [], [], [], [], []
    for l in range(depth):
        za = _mm(hb, w_a_b, l, "proj_a")
        zb = _mm(hb, w_b_b, l, "proj_b")
        zc = _mm(hb, w_c_b, l, "proj_c")

        ya, v_rows = _sgu(za, sgu_ln, a_ws, sgu_bst, l, n_p)
        yb_s, rwkv_s, sh_s = _rwkv_sample(zb, state_shift, state_rwkv_t, rwkv_s, rwkv_params, l, n_p)
        yb, r_p, sh_p = _rwkv_prompt(zb, yb_s, rwkv_params, l, batch, seq)
        yc_s, hgrn_s = _hgrn_sample(zc, c_lower_bounds, hgrn_ng, state_hgrn, hgrn_s, l, n_p)
        yc, c_p = _hgrn_prompt(zc, yc_s, c_lower_bounds, hgrn_ng, l, batch, seq)

        hb, res = _out_proj(ya, yb, yc, h, w_out_b, ln1, p_all, w_pg_b, w_pp_b, l, alpha)
        h, hb = _ffn(hb, res, w_gate_b, w_up_b, w_down_b, ln2, l)

        rwkv_p.append(r_p)
        shift_p.append(sh_p.reshape(batch, pb))
        hgrn_p.append(c_p)
        shift_s.append(sh_s)
        sgu_v.append(v_rows.reshape(n_s, 1, wa))

    return (h[:n_p].reshape(batch, seq, d), h[n_p:].reshape(n_s, 1, d), jnp.stack(rwkv_p), jnp.stack(shift_p),
            jnp.stack(hgrn_p), jnp.transpose(rwkv_s, (0, 4, 1, 2, 3)), jnp.stack(shift_s), hgrn_s, jnp.stack(sgu_v))
```

```python
import functools

import jax
import jax.numpy as jnp
from jax import lax
from jax.experimental import pallas as pl
from jax.experimental.pallas import tpu as pltpu

F32 = jnp.float32
BF16 = jnp.bfloat16

A_GROUPS = 4
A_CHUNK = 128
B_HEAD = 64
LORA_W, LORA_A, LORA_G = 64, 64, 128
C_HEAD = 128
C_CHUNK = 128
LN_EPS = 1e-5
B_GN_EPS = 1e-5 * B_HEAD
RMS_EPS = 1e-6

RWKV_CHUNK = 64
RWKV_STEP_ROWS = 128
HGRN_SUB = 16
SAMPLE_ROWS = 16
ROW_BLOCK = 128
LANES = 128
SUBLANES = 8
MXU_TILE = 256
VMEM_LIMIT_BYTES = 56 * 1024 * 1024


def _cparams(*sem):
    return pltpu.CompilerParams(dimension_semantics=sem, vmem_limit_bytes=VMEM_LIMIT_BYTES)


def _pick_tile(n, target, align):
    best = None
    for t in range(align, min(n, target) + 1, align):
        if n % t == 0:
            best = t
    assert best is not None, (n, target, align)
    return best


def _dot(a, b):
    return jnp.dot(a, b, preferred_element_type=F32)


def _dot_nt(a, b):
    return lax.dot_general(a, b, (((1,), (1,)), ((), ())), preferred_element_type=F32)


def _dot_tn(a, b):
    return lax.dot_general(a, b, (((0,), (0,)), ((), ())), preferred_element_type=F32)


def _bdot(a, b):
    return jnp.dot(a.astype(BF16), b.astype(BF16), preferred_element_type=F32)


def _bdot_nt(a, b):
    return _dot_nt(a.astype(BF16), b.astype(BF16))


def _layer_norm(x, g, b, eps):
    mu = jnp.mean(x, axis=-1, keepdims=True)
    xc = x - mu
    var = jnp.mean(xc * xc, axis=-1, keepdims=True)
    return xc * lax.rsqrt(var + eps) * g + b


def _gelu(x):
    return 0.5 * x * (1.0 + lax.erf(x * 0.7071067811865476))


def _sigmoid(x):
    return 1.0 / (1.0 + jnp.exp(-x))


def _log_sigmoid(x):
    return jnp.minimum(x, 0.0) - jnp.log1p(jnp.exp(-jnp.abs(x)))


def _split_bf16(x):
    hi = x.astype(BF16).astype(F32)
    return hi, x - hi


def _seg_sum(x, ones_bd):
    rows, width = x.shape
    t = ones_bd.shape[0]
    nb = width // t
    hi = x.astype(BF16)
    lo = (x - hi.astype(F32)).astype(BF16)
    parts = [p[:, j * t:(j + 1) * t] for p in (hi, lo) for j in range(nb)]
    r = _dot(jnp.concatenate(parts, axis=0), ones_bd)
    return jnp.concatenate(
        [r[j * rows:(j + 1) * rows] + r[(nb + j) * rows:(nb + j + 1) * rows] for j in range(nb)], axis=1)


def _cumsum_rows(x, period=None):
    rows, width = x.shape
    hi = x.astype(BF16)
    r1 = x - hi.astype(F32)
    mid = r1.astype(BF16)
    lo = (r1 - mid.astype(F32)).astype(BF16)
    ri = lax.broadcasted_iota(jnp.int32, (rows, rows), 0)
    ci = lax.broadcasted_iota(jnp.int32, (rows, rows), 1)
    tri = ri >= ci
    if period is not None and period < rows:
        tri = tri & (ci >= (ri // period) * period)
    c = _dot(jnp.where(tri, 1.0, 0.0).astype(BF16), jnp.concatenate([hi, mid, lo], axis=1))
    return c[:, :width] + c[:, width:2 * width] + c[:, 2 * width:]


def _layer_block(arr, layer):
    nd = arr.ndim - 1
    return pl.BlockSpec((None,) + arr.shape[1:], lambda *_: (layer,) + (0,) * nd)


def _whole(arr):
    nd = arr.ndim
    return pl.BlockSpec(arr.shape, lambda *_: (0,) * nd)


def _ln_in_kernel(xp_ref, xs_ref, g_ref, b_ref, h_ref, hb_ref, *, n_prompt_blocks):
    i = pl.program_id(0)

    def emit(x):
        h = _layer_norm(x, g_ref[...], b_ref[...], LN_EPS)
        h_ref[...] = h
        hb_ref[...] = h.astype(BF16)

    @pl.when(i < n_prompt_blocks)
    def _():
        emit(xp_ref[...])

    @pl.when(i >= n_prompt_blocks)
    def _():
        emit(xs_ref[...])


def _ln_in(xp, xs, g, b):
    n_p, d = xp.shape
    n_s = xs.shape[0]
    npb, nsb = n_p // ROW_BLOCK, n_s // ROW_BLOCK
    m = n_p + n_s
    return pl.pallas_call(
        functools.partial(_ln_in_kernel, n_prompt_blocks=npb),
        grid=(npb + nsb,),
        in_specs=[
            pl.BlockSpec((ROW_BLOCK, d), lambda i: (jnp.minimum(i, npb - 1), 0)),
            pl.BlockSpec((ROW_BLOCK, d), lambda i: (jnp.maximum(i - npb, 0), 0)),
            pl.BlockSpec((1, d), lambda i: (0, 0)),
            pl.BlockSpec((1, d), lambda i: (0, 0)),
        ],
        out_specs=[pl.BlockSpec((ROW_BLOCK, d), lambda i: (i, 0)), pl.BlockSpec((ROW_BLOCK, d), lambda i: (i, 0))],
        out_shape=[jax.ShapeDtypeStruct((m, d), F32), jax.ShapeDtypeStruct((m, d), BF16)],
        compiler_params=_cparams("parallel"),
        name="ln_in",
    )(xp, xs, g, b)


def _mm_kernel(x_ref, w_ref, o_ref):
    o_ref[...] = jnp.dot(x_ref[...], w_ref[...], preferred_element_type=F32)


def _mm(xb, w, layer, name):
    m, k = xb.shape
    n = w.shape[2]
    tm = _pick_tile(m, 640, 16)
    tn = _pick_tile(n, 1536, 128)
    return pl.pallas_call(
        _mm_kernel,
        grid=(m // tm, n // tn),
        in_specs=[pl.BlockSpec((tm, k), lambda i, j: (i, 0)),
                  pl.BlockSpec((None, k, tn), lambda i, j: (layer, 0, j))],
        out_specs=pl.BlockSpec((tm, tn), lambda i, j: (i, j)),
        out_shape=jax.ShapeDtypeStruct((m, n), F32),
        compiler_params=_cparams("parallel", "parallel"),
        name=name,
    )(xb, w)


def _sgu_kernel(u_ref, v_ref, ln_ref, ws_ref, bst_ref, y_ref, vn_ref, *, n_prompt_blocks):
    i = pl.program_id(0)
    gd = u_ref.shape[1] // A_GROUPS
    u = _gelu(u_ref[...])
    v = _gelu(v_ref[...])
    vn = [
        _layer_norm(v[:, g * gd:(g + 1) * gd], ln_ref[0:1, g * gd:(g + 1) * gd], ln_ref[1:2, g * gd:(g + 1) * gd], LN_EPS)
        for g in range(A_GROUPS)
    ]

    @pl.when(i < n_prompt_blocks)
    def _():
        row = lax.broadcasted_iota(jnp.int32, (A_CHUNK, A_CHUNK), 0)
        col = lax.broadcasted_iota(jnp.int32, (A_CHUNK, A_CHUNK), 1)
        for g in range(A_GROUPS):
            w_causal = jnp.where(row >= col, ws_ref[g], 0.0)
            mixed = _bdot(w_causal, vn[g]) + bst_ref[:, g:g + 1]
            y_ref[:, g * gd:(g + 1) * gd] = (u[:, g * gd:(g + 1) * gd] * mixed).astype(BF16)

    @pl.when(i >= n_prompt_blocks)
    def _():
        for g in range(A_GROUPS):
            mixed = vn[g] * ws_ref[g, 0:1, 0:1] + bst_ref[0:1, g:g + 1]
            y_ref[:, g * gd:(g + 1) * gd] = (u[:, g * gd:(g + 1) * gd] * mixed).astype(BF16)
            vn_ref[:, g * gd:(g + 1) * gd] = vn[g]


def _sgu(za, ln, ws, bst, layer, n_prompt):
    m = za.shape[0]
    wa = za.shape[1] // 2
    npb = n_prompt // A_CHUNK
    nb = m // A_CHUNK
    n_s = m - n_prompt
    return pl.pallas_call(
        functools.partial(_sgu_kernel, n_prompt_blocks=npb),
        grid=(nb,),
        in_specs=[
            pl.BlockSpec((A_CHUNK, wa), lambda i: (i, 0)),
            pl.BlockSpec((A_CHUNK, wa), lambda i: (i, 1)),
            _layer_block(ln, layer), _layer_block(ws, layer), _layer_block(bst, layer),
        ],
        out_specs=[
            pl.BlockSpec((A_CHUNK, wa), lambda i: (i, 0)),
            pl.BlockSpec((A_CHUNK, wa), lambda i: (jnp.maximum(i - npb, 0), 0)),
        ],
        out_shape=[jax.ShapeDtypeStruct((m, wa), BF16), jax.ShapeDtypeStruct((n_s, wa), F32)],
        compiler_params=_cparams("arbitrary"),
        name="sgu",
    )(za, za, ln, ws, bst)


_V_W0, _V_A0, _V_KK, _V_KA, _V_RK, _V_GNG, _V_GNB = range(7)


def _rwkv_prep(zb, prev, mu, vec, wup, aup, gup, ones_bd):
    wb = vec.shape[1]
    row = lambda j: vec[j:j + 1, :]
    xs = zb + mu * (prev - zb)
    r, k, v = xs[:, :wb], xs[:, wb:2 * wb], xs[:, 2 * wb:3 * wb]
    o4, o5 = 3 * wb + LORA_W, 3 * wb + LORA_W + LORA_A
    wd, ad, gd = xs[:, 3 * wb:o4], xs[:, o4:o5], xs[:, o5:]
    w = row(_V_W0) + _bdot(jnp.tanh(wd), wup)
    softplus_neg_w = jnp.maximum(-w, 0.0) + jnp.log1p(jnp.exp(-jnp.abs(w)))
    log_decay = -jnp.exp(-softplus_neg_w - 0.5)
    a = _sigmoid(row(_V_A0) + _bdot(ad, aup))
    g = _bdot(_sigmoid(gd), gup)
    kk = k * row(_V_KK)
    kk = kk / jnp.maximum(jnp.sqrt(_seg_sum(kk * kk, ones_bd)), 1e-12)
    kd = k * (1.0 + (a - 1.0) * row(_V_KA))
    bonus = _seg_sum(r * kd * row(_V_RK), ones_bd) * v
    return r, log_decay, kd, v, kk, a, g, bonus


def _rwkv_finish(y, bonus, g, vec, ones_bd):
    mean = _seg_sum(y, ones_bd) * (1.0 / B_HEAD)
    yc = y - mean
    var = _seg_sum(yc * yc, ones_bd) * (1.0 / B_HEAD)
    return (yc * lax.rsqrt(var + B_GN_EPS) * vec[_V_GNG:_V_GNG + 1, :] + vec[_V_GNB:_V_GNB + 1, :] + bonus) * g


def _rwkv_prompt_kernel(zb_ref, ys_ref, mu_ref, vec_ref, wup_ref, aup_ref, gup_ref, ones_ref,
                        y_ref, sout_ref, shift_ref, state_scr, prev_scr, *, steps_per_seq, n_prompt_steps):
    i = pl.program_id(0)

    @pl.when(i >= n_prompt_steps)
    def _():
        y_ref[...] = ys_ref[...]

    @pl.when(i < n_prompt_steps)
    def _():
        s_idx = lax.rem(i, steps_per_seq)

        @pl.when(s_idx == 0)
        def _():
            state_scr[...] = jnp.zeros_like(state_scr)
            prev_scr[...] = jnp.zeros_like(prev_scr)

        zb = zb_ref[...]
        R = zb.shape[0]
        C = RWKV_CHUNK
        n_pairs = state_scr.shape[0]
        N = B_HEAD
        P = 2 * N
        ones_bd = ones_ref[...]
        vec = vec_ref[...]
        row1 = lax.broadcasted_iota(jnp.int32, (R, 1), 0)
        prev = jnp.where(row1 == 0, prev_scr[...], pltpu.roll(zb, 1, 0))
        prev_scr[...] = zb[R - 1:R, :]
        r_all, lw_all, kd_all, v_all, kk_all, a_all, g, bonus = _rwkv_prep(
            zb, prev, mu_ref[...], vec, wup_ref[...], aup_ref[...], gup_ref[...], ones_bd)
        cum_all = _cumsum_rows(lw_all, C)

        lane = lax.broadcasted_iota(jnp.int32, (C, P), 1)
        rowc = lax.broadcasted_iota(jnp.int32, (C, P), 0)
        lo = lane < N
        col_in = jnp.where(lo, lane, lane - N)
        incl2 = rowc >= col_in
        strict2 = rowc > col_in
        lo2 = lax.broadcasted_iota(jnp.int32, (2 * C, P), 1) < N
        bd_mask = (lax.broadcasted_iota(jnp.int32, (P, P), 0) < N) == (lax.broadcasted_iota(jnp.int32, (P, P), 1) < N)
        pairs = [slice(p * P, (p + 1) * P) for p in range(n_pairs)]

        def chunk_factors(c):
            rs = slice(c * C, (c + 1) * C)
            r, lw, kd, v, kk, a, cum = (t[rs] for t in (r_all, lw_all, kd_all, v_all, kk_all, a_all, cum_all))
            m = cum[C // 2 - 1:C // 2, :]
            cend = cum[C - 1:C, :]
            e_pos = jnp.exp(cum - m)
            e_neg = jnp.exp(m - cum)
            e_prev = jnp.exp(cum - lw - m)
            e_m = jnp.exp(m)
            e_end = jnp.exp(cend - m)
            rp = r * e_pos
            kkp = kk * e_prev
            kp = kd * e_neg
            bp = kk * a * e_neg
            rhat = rp * e_m
            kkhat = kkp * e_m
            ktil = kp * e_end
            btil = bp * e_end
            e_cend = e_end * e_m
            return dict(rp=rp, kkp=kkp, kp=kp, bp=bp, rhat=rhat, kkhat=kkhat, ktil=ktil, btil=btil, v=v,
                        e_cend=e_cend)

        def local_units(facs):
            units = [(f, ps) for f in facs for ps in pairs]
            scores = []
            for f, ps in units:
                q_pair = jnp.concatenate([f["rp"][:, ps], f["kkp"][:, ps]], axis=0)
                q1 = jnp.where(lo2, q_pair, 0.0).astype(BF16)
                q2 = jnp.where(lo2, 0.0, q_pair).astype(BF16)
                kp_b, bp_b = f["kp"][:, ps].astype(BF16), f["bp"][:, ps].astype(BF16)
                scores.append((_dot_nt(q1, jnp.concatenate([bp_b, kp_b], axis=0)),
                               _dot_nt(q2, jnp.concatenate([kp_b, bp_b], axis=0))))
            ar1, ar2, n_bd, avs, v_b = [], [], [], [], []
            for (f, ps), (sc1, sc2) in zip(units, scores):
                ar1.append(jnp.where(incl2, sc1[:C], 0.0).astype(BF16))
                ar2.append(jnp.where(incl2, sc2[:C], 0.0).astype(BF16))
                kn1 = jnp.where(strict2, sc1[C:], 0.0)
                kn2 = jnp.where(strict2, sc2[C:], 0.0)
                n_bd.append(jnp.concatenate([jnp.where(lo, kn1, 0.0), jnp.where(lo, 0.0, kn2)], axis=0).astype(BF16))
                akk = jnp.concatenate([jnp.where(lo, 0.0, kn1), jnp.where(lo, kn2, 0.0)], axis=0).astype(BF16)
                v_pair = f["v"][:, ps]
                v_b.append(v_pair.astype(BF16))
                v_sw = pltpu.roll(v_pair, N, 1).astype(BF16)
                avs.append(_dot(akk, jnp.concatenate([v_sw, v_sw], axis=0)))
            xs = []
            for (f, ps), av in zip(units, avs):
                kkh = f["kkhat"][:, ps]
                xs.append(jnp.concatenate([jnp.where(lo, kkh, av[:C]), jnp.where(lo, av[C:], kkh)], axis=0))
            xs = [x - _dot(nb, x.astype(BF16)) for nb, x in zip(n_bd, xs)]
            pw = n_bd
            span = 2
            while span < C:
                pw = [_dot(t, t).astype(BF16) for t in pw]
                xs = [x + _dot(t, x.astype(BF16)) for t, x in zip(pw, xs)]
                span *= 2
            zeros_cp = jnp.zeros((C, P), BF16)
            zs, w_parts = [], []
            for (f, ps), x, a1, a2, vb in zip(units, xs, ar1, ar2, v_b):
                w1 = jnp.where(lo, x[:C], x[C:])
                w2 = pltpu.roll(jnp.where(lo, x[C:], x[:C]), N, 1)
                w1_hi, w1_lo = _split_bf16(w1)
                w2_hi, w2_lo = _split_bf16(w2)
                nw = jnp.concatenate([-w1_hi, -w2_hi], axis=1).astype(BF16)
                zv = jnp.concatenate([zeros_cp, vb], axis=1)
                zs.append((_dot(a1, jnp.concatenate([nw, zv], axis=0)),
                           _dot(a2, jnp.concatenate([zv, nw], axis=0))))
                w_parts.append((w1_hi, w1_lo, w2_hi, w2_lo))
            out = []
            for (f, ps), (z1, z2), (w1_hi, w1_lo, w2_hi, w2_lo), vb in zip(units, zs, w_parts, v_b):
                qeff = (f["rhat"][:, ps] + jnp.where(lo, z1[:, :P], z2[:, :P])).astype(BF16)
                yloc = jnp.where(lo, z1[:, P:], z2[:, P:])
                kt_b, bt_b = f["ktil"][:, ps].astype(BF16), f["btil"][:, ps].astype(BF16)
                w1tb = _dot_tn(jnp.concatenate([w1_hi, w1_lo], axis=0).astype(BF16),
                               jnp.concatenate([bt_b, bt_b], axis=0))
                hm = _dot_tn(jnp.concatenate([vb, (-w2_hi).astype(BF16), (-w2_lo).astype(BF16)], axis=0),
                             jnp.concatenate([kt_b, bt_b, bt_b], axis=0))
                out.append(dict(qeff=qeff, yloc=yloc, g_bd=jnp.where(bd_mask, -w1tb, 0.0).astype(BF16),
                                h_bd=jnp.where(bd_mask, hm, 0.0), e_cend=f["e_cend"][:, ps]))
            return [out[c * n_pairs:(c + 1) * n_pairs] for c in range(len(facs))]

        def state_part(loc, states):
            ys, new_states = [], []
            for u, s_bd in zip(loc, states):
                s_hi, s_lo = _split_bf16(s_bd)
                s_hi_b = s_hi.astype(BF16)
                ys.append(_dot_nt(u["qeff"], s_hi_b) + u["yloc"])
                g2 = jnp.concatenate([u["g_bd"], u["g_bd"]], axis=0)
                sg = _dot(jnp.concatenate([s_hi_b, s_lo.astype(BF16)], axis=1), g2)
                new_states.append(s_bd * u["e_cend"] + sg + u["h_bd"])
            return jnp.concatenate(ys, axis=1), new_states

        locs = local_units([chunk_factors(c) for c in range(R // C)])
        states = [state_scr[p] for p in range(n_pairs)]
        y_rows = []
        for loc in locs:
            y_c, states = state_part(loc, states)
            y_rows.append(y_c)
        for p in range(n_pairs):
            state_scr[p] = states[p]

        y = jnp.concatenate(y_rows, axis=0)
        y_ref[...] = _rwkv_finish(y, bonus, g, vec, ones_bd).astype(BF16)

        @pl.when(s_idx == steps_per_seq - 1)
        def _():
            shift_ref[0] = zb[R - 1:R, :]
            for p in range(n_pairs):
                sout_ref[0, 2 * p] = states[p][:N, :N]
                sout_ref[0, 2 * p + 1] = states[p][N:, N:]


def _rwkv_prompt(zb, y_sample, params, layer, batch, seq):
    mu, vec, wup, aup, gup, ones_bd = params
    pb = zb.shape[1]
    wb = vec.shape[2]
    n_heads = wb // B_HEAD
    R = RWKV_STEP_ROWS
    sps = seq // R
    n_steps = batch * sps
    n_s = y_sample.shape[0]
    m = batch * seq + n_s
    seq_of = lambda i: jnp.minimum(i // sps, batch - 1)
    return pl.pallas_call(
        functools.partial(_rwkv_prompt_kernel, steps_per_seq=sps, n_prompt_steps=n_steps),
        grid=(n_steps + n_s // R,),
        in_specs=[
            pl.BlockSpec((R, pb), lambda i: (jnp.minimum(i, n_steps - 1), 0)),
            pl.BlockSpec((R, wb), lambda i: (jnp.maximum(i - n_steps, 0), 0)),
            _layer_block(mu, layer), _layer_block(vec, layer), _layer_block(wup, layer),
            _layer_block(aup, layer), _layer_block(gup, layer), _whole(ones_bd),
        ],
        out_specs=[
            pl.BlockSpec((R, wb), lambda i: (i, 0)),
            pl.BlockSpec((1, n_heads, B_HEAD, B_HEAD), lambda i: (seq_of(i), 0, 0, 0)),
            pl.BlockSpec((1, 1, pb), lambda i: (seq_of(i), 0, 0)),
        ],
        out_shape=[
            jax.ShapeDtypeStruct((m, wb), BF16),
            jax.ShapeDtypeStruct((batch, n_heads, B_HEAD, B_HEAD), F32),
            jax.ShapeDtypeStruct((batch, 1, pb), F32),
        ],
        scratch_shapes=[pltpu.VMEM((n_heads // 2, 2 * B_HEAD, 2 * B_HEAD), F32), pltpu.VMEM((1, pb), F32)],
        compiler_params=_cparams("arbitrary"),
        name="rwkv_prompt",
    )(zb, y_sample, mu, vec, wup, aup, gup, ones_bd)


def _rwkv_sample_kernel(zb_ref, shift_ref, mu_ref, vec_ref, wup_ref, aup_ref, gup_ref, ones_ref, s_ref, buf_ref,
                        y_ref, sout_ref, shift_out_ref,
                        kk_t, wr_t, b_t, kd_t, v_t, w_t, r_t, y_t, g_scr, bonus_scr):
    del buf_ref
    p = pl.program_id(1)
    heads_per_step = s_ref.shape[0]
    N = B_HEAD

    @pl.when(p == 0)
    def _():
        zb = zb_ref[...]
        shift_out_ref[...] = zb
        r, lw, kd, v, kk, a, g, bonus = _rwkv_prep(
            zb, shift_ref[...], mu_ref[...], vec_ref[...], wup_ref[...], aup_ref[...], gup_ref[...], ones_ref[...])
        w = jnp.exp(lw)
        kk_t[...] = kk.T
        wr_t[...] = (w * r).T
        b_t[...] = (kk * a).T
        kd_t[...] = kd.T
        v_t[...] = v.T
        w_t[...] = w.T
        r_t[...] = r.T
        g_scr[...] = g
        bonus_scr[...] = bonus

    for hh in range(heads_per_step):
        base = pl.multiple_of((p * heads_per_step + hh) * N, N)
        hs = pl.ds(base, N)
        kk_h, wr_h, b_h, kd_h, w_h, r_h = kk_t[hs, :], wr_t[hs, :], b_t[hs, :], kd_t[hs, :], w_t[hs, :], r_t[hs, :]
        b_dot_r = jnp.sum(b_h * r_h, axis=0, keepdims=True)
        k_dot_r = jnp.sum(kd_h * r_h, axis=0, keepdims=True)

        def body(vi, carry):
            s0 = s_ref[hh, vi]
            s_kk = jnp.sum(s0 * kk_h, axis=0, keepdims=True)
            yq = jnp.sum(s0 * wr_h, axis=0, keepdims=True)
            v_row = v_t[pl.ds(base + vi, 1), :]
            sout_ref[hh, vi] = s0 * w_h - s_kk * b_h + v_row * kd_h
            y_t[pl.ds(base + vi, 1), :] = yq - s_kk * b_dot_r + v_row * k_dot_r
            return carry

        lax.fori_loop(0, N, body, 0, unroll=4)

    @pl.when(p == pl.num_programs(1) - 1)
    def _():
        y = y_t[...].T
        y_ref[...] = _rwkv_finish(y, bonus_scr[...], g_scr[...], vec_ref[...], ones_ref[...]).astype(BF16)


def _rwkv_sample(zb, shift, state_t, out_buf, params, layer, n_prompt):
    mu, vec, wup, aup, gup, ones_bd = params
    pb = zb.shape[1]
    wb = vec.shape[2]
    n_heads, n_s = state_t.shape[1], state_t.shape[4]
    hps = 2
    off = n_prompt // LANES
    sspec = pl.BlockSpec((None, hps, B_HEAD, B_HEAD, LANES), lambda sb, p: (layer, p, 0, 0, sb))
    return pl.pallas_call(
        _rwkv_sample_kernel,
        grid=(n_s // LANES, n_heads // hps),
        in_specs=[
            pl.BlockSpec((LANES, pb), lambda sb, p: (off + sb, 0)),
            pl.BlockSpec((None, LANES, pb), lambda sb, p: (layer, sb, 0)),
            _layer_block(mu, layer), _layer_block(vec, layer), _layer_block(wup, layer),
            _layer_block(aup, layer), _layer_block(gup, layer), _whole(ones_bd),
            sspec, pl.BlockSpec(memory_space=pl.ANY),
        ],
        out_specs=[pl.BlockSpec((LANES, wb), lambda sb, p: (sb, 0)), sspec,
                   pl.BlockSpec((LANES, pb), lambda sb, p: (sb, 0))],
        out_shape=[jax.ShapeDtypeStruct((n_s, wb), BF16), jax.ShapeDtypeStruct(state_t.shape, F32),
                   jax.ShapeDtypeStruct((n_s, pb), F32)],
        input_output_aliases={9: 1},
        scratch_shapes=[pltpu.VMEM((wb, LANES), F32)] * 8 + [pltpu.VMEM((LANES, wb), F32)] * 2,
        compiler_params=_cparams("arbitrary", "arbitrary"),
        name="rwkv_sample",
    )(zb, shift, mu, vec, wup, aup, gup, ones_bd, state_t, out_buf)


def _hgrn_gates(fz, clb, layer):
    tail = jnp.log1p(jnp.exp(-jnp.abs(fz)))
    ls_pos = jnp.minimum(fz, 0.0) - tail
    ls_neg = jnp.minimum(-fz, 0.0) - tail
    sig_neg = jnp.exp(ls_neg)
    if layer == 0:
        return ls_pos, sig_neg
    e = jnp.exp(clb - jnp.max(clb, axis=0, keepdims=True))
    sm = e / jnp.sum(e, axis=0, keepdims=True)
    lb = jnp.sum(sm[1:layer + 1], axis=0, keepdims=True)
    x2 = jnp.log(lb) + ls_neg
    log_f = jnp.maximum(ls_pos, x2) + jnp.log1p(jnp.exp(-jnp.abs(ls_pos - x2)))
    return log_f, (1.0 - lb) * sig_neg


def _hgrn_finish(o, g, norm_g):
    on = o * lax.rsqrt(jnp.mean(o * o, axis=-1, keepdims=True) + RMS_EPS) * norm_g
    return on * (g * _sigmoid(g))


def _hgrn_prompt_kernel(zc_ref, ys_ref, clb_ref, ng_ref, y_ref, sout_ref, state_scr, *, layer, n_chunks, n_prompt_steps):
    i = pl.program_id(0)

    @pl.when(i >= n_prompt_steps)
    def _():
        y_ref[...] = ys_ref[...]

    @pl.when(i < n_prompt_steps)
    def _():
        c_idx = lax.rem(i, n_chunks)

        @pl.when(c_idx == 0)
        def _():
            state_scr[...] = jnp.zeros_like(state_scr)

        z = zc_ref[...]
        C = z.shape[0]
        wc = z.shape[1] // 4
        n_heads = wc // C_HEAD
        D = C_HEAD
        SUB = HGRN_SUB
        HALF = SUBLANES
        nsub = C // SUB
        q, fz, iv, g = z[:, :wc], z[:, wc:2 * wc], z[:, 2 * wc:3 * wc], z[:, 3 * wc:]
        log_f, kg = _hgrn_gates(fz, clb_ref[...], layer)
        bcum = _cumsum_rows(log_f)
        e_b = jnp.exp(bcum)
        b_end = bcum[C - 1:C, :]
        e_end = jnp.exp(b_end)
        k_hat = kg * jnp.exp(b_end - bcum)
        t_full = lax.broadcasted_iota(jnp.int32, (nsub, SUB, 1), 1)
        t_half = lax.broadcasted_iota(jnp.int32, (nsub, SUB - HALF, 1), 1) + HALF
        level_sizes = [SUB << li for li in range((C // SUB).bit_length() - 1)]
        half = C // 2
        n_lv = len(level_sizes) * half
        ri = lax.broadcasted_iota(jnp.int32, (n_lv, n_lv), 0)
        ci = lax.broadcasted_iota(jnp.int32, (n_lv, n_lv), 1)
        level_mask = None
        for li, s in enumerate(level_sizes):
            shift = s.bit_length() - 1
            in_level = (ri >= li * half) & (ri < (li + 1) * half) & (ci >= li * half) & (ci < (li + 1) * half)
            same_block = ((ri - li * half) >> shift) == ((ci - li * half) >> shift)
            lm = in_level & same_block
            level_mask = lm if level_mask is None else (level_mask | lm)

        outs = []
        for h in range(n_heads):
            sl = slice(h * D, (h + 1) * D)
            qh, kh, vh, bh = q[:, sl], kg[:, sl], iv[:, sl], bcum[:, sl]
            st = state_scr[h]
            o = _bdot_nt(qh * e_b[:, sl], st)
            q3, k3, v3, b3 = (t.reshape(nsub, SUB, D) for t in (qh, kh, vh, bh))
            q3h, b3h = q3[:, HALF:, :], b3[:, HALF:, :]
            od = jnp.zeros((nsub, SUB, D), F32)
            odh = jnp.zeros((nsub, SUB - HALF, D), F32)
            for j in range(SUB):
                if j < HALF:
                    dec = jnp.exp(b3 - b3[:, j:j + 1, :])
                    att = jnp.sum(q3 * k3[:, j:j + 1, :] * dec, axis=-1, keepdims=True)
                    od = od + jnp.where(t_full >= j, att, 0.0) * v3[:, j:j + 1, :]
                else:
                    dec = jnp.exp(b3h - b3[:, j:j + 1, :])
                    att = jnp.sum(q3h * k3[:, j:j + 1, :] * dec, axis=-1, keepdims=True)
                    odh = odh + jnp.where(t_half >= j, att, 0.0) * v3[:, j:j + 1, :]
            od = od + jnp.concatenate([jnp.zeros((nsub, HALF, D), F32), odh], axis=1)
            o = o + od.reshape(C, D)
            q_l, k_l, v_l = [], [], []
            for s in level_sizes:
                for j in range(C // (2 * s)):
                    a0 = 2 * j * s
                    b_bound = bh[a0 + s - 1:a0 + s, :]
                    q_l.append(qh[a0 + s:a0 + 2 * s, :] * jnp.exp(bh[a0 + s:a0 + 2 * s, :] - b_bound))
                    k_l.append(kh[a0:a0 + s, :] * jnp.exp(b_bound - bh[a0:a0 + s, :]))
                    v_l.append(vh[a0:a0 + s, :])
            att = _bdot_nt(jnp.concatenate(q_l, axis=0), jnp.concatenate(k_l, axis=0))
            o_lv = _bdot(jnp.where(level_mask, att, 0.0), jnp.concatenate(v_l, axis=0))
            for li, s in enumerate(level_sizes):
                pieces = []
                for j in range(C // (2 * s)):
                    r0 = li * (C // 2) + j * s
                    pieces += [jnp.zeros((s, D), F32), o_lv[r0:r0 + s, :]]
                o = o + jnp.concatenate(pieces, axis=0)
            outs.append(o)
            vh_hi, vh_lo = _split_bf16(vh)
            kh_hi, kh_lo = _split_bf16(k_hat[:, sl])
            upd = _dot_tn(jnp.concatenate([vh_hi, vh_hi, vh_lo], axis=0).astype(BF16),
                          jnp.concatenate([kh_hi, kh_lo, kh_hi], axis=0).astype(BF16))
            state_scr[h] = st * e_end[:, sl] + upd

        o_all = jnp.concatenate(outs, axis=1)
        y_ref[...] = _hgrn_finish(o_all, g, ng_ref[...]).astype(BF16)

        @pl.when(c_idx == n_chunks - 1)
        def _():
            for h in range(n_heads):
                sout_ref[0, h] = state_scr[h].T


def _hgrn_prompt(zc, y_sample, clb, norm_g, layer, batch, seq):
    pc = zc.shape[1]
    wc = pc // 4
    n_heads = wc // C_HEAD
    C = C_CHUNK
    nc = seq // C
    n_steps = batch * nc
    n_s = y_sample.shape[0]
    m = batch * seq + n_s
    return pl.pallas_call(
        functools.partial(_hgrn_prompt_kernel, layer=layer, n_chunks=nc, n_prompt_steps=n_steps),
        grid=(n_steps + n_s // C,),
        in_specs=[
            pl.BlockSpec((C, pc), lambda i: (jnp.minimum(i, n_steps - 1), 0)),
            pl.BlockSpec((C, wc), lambda i: (jnp.maximum(i - n_steps, 0), 0)),
            _whole(clb), _layer_block(norm_g, layer),
        ],
        out_specs=[
            pl.BlockSpec((C, wc), lambda i: (i, 0)),
            pl.BlockSpec((1, n_heads, C_HEAD, C_HEAD), lambda i: (jnp.minimum(i // nc, batch - 1), 0, 0, 0)),
        ],
        out_shape=[
            jax.ShapeDtypeStruct((m, wc), BF16),
            jax.ShapeDtypeStruct((batch, n_heads, C_HEAD, C_HEAD), F32),
        ],
        scratch_shapes=[pltpu.VMEM((n_heads, C_HEAD, C_HEAD), F32)],
        compiler_params=_cparams("arbitrary"),
        name="hgrn_prompt",
    )(zc, y_sample, clb, norm_g)


def _hgrn_sample_kernel(zc_ref, clb_ref, ng_ref, s_ref, buf_ref, y_ref, sout_ref,
                        q_scr, f_scr, k_scr, v_scr, o_scr, *, layer):
    del buf_ref
    z = zc_ref[...]
    n_rows, n_heads = s_ref.shape[0], s_ref.shape[1]
    wc = z.shape[1] // 4
    D = C_HEAD
    q, fz, iv, g = z[:, :wc], z[:, wc:2 * wc], z[:, 2 * wc:3 * wc], z[:, 3 * wc:]
    log_f, kg = _hgrn_gates(fz, clb_ref[...], layer)
    q_scr[...] = q
    f_scr[...] = jnp.exp(log_f)
    k_scr[...] = kg
    v_scr[...] = iv
    pad = jnp.zeros((SUBLANES - n_heads, D), F32)

    def body(i, carry):
        q_r, f_r, k_r, v_r = (t[pl.ds(i, 1), :] for t in (q_scr, f_scr, k_scr, v_scr))
        heads = lambda t: jnp.concatenate([t[:, h * D:(h + 1) * D] for h in range(n_heads)] + [pad], axis=0)
        f_cols = heads(f_r).T
        k_cols = heads(k_r).T
        os_ = []
        for h in range(n_heads):
            sl = slice(h * D, (h + 1) * D)
            s0 = s_ref[i, h]
            qk = jnp.sum(q_r[:, sl] * k_r[:, sl], axis=-1, keepdims=True)
            os_.append(_bdot(q_r[:, sl] * f_r[:, sl], s0) + qk * v_r[:, sl])
            sout_ref[i, h] = s0 * f_cols[:, h:h + 1] + k_cols[:, h:h + 1] * v_r[:, sl]
        o_scr[pl.ds(i, 1), :] = jnp.concatenate(os_, axis=1)
        return carry

    lax.fori_loop(0, n_rows, body, 0)
    y_ref[...] = _hgrn_finish(o_scr[...], g, ng_ref[...]).astype(BF16)


def _hgrn_sample(zc, clb, norm_g, state, out_buf, layer, n_prompt):
    pc = zc.shape[1]
    wc = pc // 4
    n_s, n_heads = state.shape[1], state.shape[2]
    R = SAMPLE_ROWS
    off = n_prompt // R
    sspec = pl.BlockSpec((None, R, n_heads, C_HEAD, C_HEAD), lambda i: (layer, i, 0, 0, 0))
    return pl.pallas_call(
        functools.partial(_hgrn_sample_kernel, layer=layer),
        grid=(n_s // R,),
        in_specs=[pl.BlockSpec((R, pc), lambda i: (off + i, 0)), _whole(clb), _layer_block(norm_g, layer),
                  sspec, pl.BlockSpec(memory_space=pl.ANY)],
        out_specs=[pl.BlockSpec((R, wc), lambda i: (i, 0)), sspec],
        out_shape=[jax.ShapeDtypeStruct((n_s, wc), BF16), jax.ShapeDtypeStruct(state.shape, F32)],
        input_output_aliases={4: 1},
        scratch_shapes=[pltpu.VMEM((R, wc), F32)] * 5,
        compiler_params=_cparams("parallel"),
        name="hgrn_sample",
    )(zc, clb, norm_g, state, out_buf)


def _out_kernel(ya_ref, yb_ref, yc_ref, h_ref, wo_ref, ln_ref, p_ref, wpg_ref, wpp_ref, hb_ref, res_ref, *, alpha):
    wa, wb = ya_ref.shape[1], yb_ref.shape[1]
    mix = (_dot(ya_ref[...], wo_ref[:wa, :]) + _dot(yb_ref[...], wo_ref[wa:wa + wb, :])
           + _dot(yc_ref[...], wo_ref[wa + wb:, :]))
    h1 = _layer_norm(alpha * h_ref[...] + mix, ln_ref[0:1, :], ln_ref[1:2, :], LN_EPS)
    h1b = h1.astype(BF16)
    ple = _sigmoid(_dot(h1b, wpg_ref[...])) * _dot(p_ref[...], wpp_ref[...])
    hb_ref[...] = h1b
    res_ref[...] = alpha * h1 + ple


def _out_proj(ya, yb, yc, h, wo, ln, p, wpg, wpp, layer, alpha):
    m, d = h.shape
    tm = _pick_tile(m, 320, 16)
    rows = lambda a: pl.BlockSpec((tm, a.shape[-1]), lambda i: (i, 0))
    const = lambda a: pl.BlockSpec((None,) + a.shape[1:], lambda i: (layer, 0, 0), pipeline_mode=pl.Buffered(1))
    return pl.pallas_call(
        functools.partial(_out_kernel, alpha=alpha),
        grid=(m // tm,),
        in_specs=[rows(ya), rows(yb), rows(yc), rows(h), const(wo), const(ln),
                  pl.BlockSpec((None, tm, p.shape[-1]), lambda i: (layer, i, 0)), const(wpg), const(wpp)],
        out_specs=[rows(h), rows(h)],
        out_shape=[jax.ShapeDtypeStruct((m, d), BF16), jax.ShapeDtypeStruct((m, d), F32)],
        compiler_params=_cparams("parallel"),
        name="out_proj",
    )(ya, yb, yc, h, wo, ln, p, wpg, wpp)


def _ffn_kernel(hb_ref, res_ref, wg_ref, wu_ref, wd_ref, ln_ref, h_ref, hbo_ref, acc_ref):
    f = pl.program_id(1)

    @pl.when(f == 0)
    def _():
        acc_ref[...] = jnp.zeros_like(acc_ref)

    x = hb_ref[...]
    gate = _dot(x, wg_ref[...])
    up = _dot(x, wu_ref[...])
    act = (gate * _sigmoid(gate) * up).astype(BF16)
    acc_ref[...] += _dot(act, wd_ref[...])

    @pl.when(f == pl.num_programs(1) - 1)
    def _():
        h2 = _layer_norm(res_ref[...] + acc_ref[...], ln_ref[0:1, :], ln_ref[1:2, :], LN_EPS)
        h_ref[...] = h2
        hbo_ref[...] = h2.astype(BF16)


def _ffn(hb, res, wg, wu, wd, ln, layer):
    m, d = hb.shape
    dff = wg.shape[2]
    tm = _pick_tile(m, 640, 16)
    tf = _pick_tile(dff, 512, 128)
    return pl.pallas_call(
        _ffn_kernel,
        grid=(m // tm, dff // tf),
        in_specs=[
            pl.BlockSpec((tm, d), lambda i, f: (i, 0)),
            pl.BlockSpec((tm, d), lambda i, f: (i, 0)),
            pl.BlockSpec((None, d, tf), lambda i, f: (layer, 0, f)),
            pl.BlockSpec((None, d, tf), lambda i, f: (layer, 0, f)),
            pl.BlockSpec((None, tf, d), lambda i, f: (layer, f, 0)),
            pl.BlockSpec((None, 2, d), lambda i, f: (layer, 0, 0)),
        ],
        out_specs=[pl.BlockSpec((tm, d), lambda i, f: (i, 0)), pl.BlockSpec((tm, d), lambda i, f: (i, 0))],
        out_shape=[jax.ShapeDtypeStruct((m, d), F32), jax.ShapeDtypeStruct((m, d), BF16)],
        scratch_shapes=[pltpu.VMEM((tm, d), F32)],
        compiler_params=_cparams("parallel", "arbitrary"),
        name="ffn",
    )(hb, res, wg, wu, wd, ln)


def kernel(x_prompt, x_sample, state_rwkv, state_shift, state_hgrn, p_prompt, p_sample, ln_in_g, ln_in_b, w_in, a_ln_g, a_ln_b, a_ws, a_bs, b_mu, b_w0, b_w_up, b_a0, b_a_up, b_g_up, b_k_k, b_k_a, b_r_k, b_gn_g, b_gn_b, c_lower_bounds, c_norm_g, w_out, ln1_g, ln1_b, w_ffn_gate, w_ffn_up, w_ffn_down, w_ple_gate, w_ple_proj, ln2_g, ln2_b):
    batch, seq, d = x_prompt.shape
    n_s = x_sample.shape[0]
    depth = w_in.shape[0]
    n_p = batch * seq
    wa, wb, wc = a_ln_g.shape[1], b_w0.shape[1], c_norm_g.shape[1]
    pa, pb = 2 * wa, 3 * wb + LORA_W + LORA_A + LORA_G
    alpha = float((2 * depth) ** 0.25)
    assert x_sample.shape[1] == 1 and seq % A_CHUNK == 0 and seq % C_CHUNK == 0 and seq % RWKV_STEP_ROWS == 0
    assert n_s % ROW_BLOCK == 0 and n_p % ROW_BLOCK == 0 and n_s % LANES == 0
    assert (wb // B_HEAD) % 2 == 0 and wb % MXU_TILE == 0

    bf = lambda t: t.astype(BF16)
    w_a_b, w_b_b, w_c_b = bf(w_in[:, :, :pa]), bf(w_in[:, :, pa:pa + pb]), bf(w_in[:, :, pa + pb:])
    w_out_b, w_gate_b, w_up_b, w_down_b = bf(w_out), bf(w_ffn_gate), bf(w_ffn_up), bf(w_ffn_down)
    w_pg_b, w_pp_b = bf(w_ple_gate), bf(w_ple_proj)
    p_all = bf(jnp.concatenate([p_prompt.reshape(depth, n_p, -1), p_sample.reshape(depth, n_s, -1)], axis=1))
    sgu_ln = jnp.stack([a_ln_g, a_ln_b], axis=1)
    sgu_bst = jnp.swapaxes(a_bs, 1, 2)
    rwkv_vec = jnp.stack([b_w0, b_a0, b_k_k, b_k_a, b_r_k.reshape(depth, wb), b_gn_g, b_gn_b, jnp.zeros_like(b_w0)], axis=1)
    idx = jnp.arange(MXU_TILE)
    ones_bd = bf(idx[:, None] // B_HEAD == idx[None, :] // B_HEAD)
    rwkv_params = (b_mu.reshape(depth, 1, pb), rwkv_vec, bf(b_w_up), bf(b_a_up), bf(b_g_up), ones_bd)
    hgrn_ng = c_norm_g.reshape(depth, 1, wc)
    ln1 = jnp.stack([ln1_g, ln1_b], axis=1)
    ln2 = jnp.stack([ln2_g, ln2_b], axis=1)
    row = lambda t: t.reshape(1, -1)

    h, hb = _ln_in(x_prompt.reshape(n_p, d), x_sample.reshape(n_s, d), row(ln_in_g), row(ln_in_b))

    state_rwkv_t = jnp.transpose(state_rwkv, (0, 2, 3, 4, 1))
    rwkv_s = jnp.zeros(state_rwkv_t.shape, F32)
    hgrn_s = jnp.zeros(state_hgrn.shape, F32)
    rwkv_p, shift_p, hgrn_p, shift_s, sgu_v = [], [], [], [], []
    for l in range(depth):
        za = _mm(hb, w_a_b, l, "proj_a")
        zb = _mm(hb, w_b_b, l, "proj_b")
        zc = _mm(hb, w_c_b, l, "proj_c")

        ya, v_rows = _sgu(za, sgu_ln, a_ws, sgu_bst, l, n_p)
        yb_s, rwkv_s, sh_s = _rwkv_sample(zb, state_shift, state_rwkv_t, rwkv_s, rwkv_params, l, n_p)
        yb, r_p, sh_p = _rwkv_prompt(zb, yb_s, rwkv_params, l, batch, seq)
        yc_s, hgrn_s = _hgrn_sample(zc, c_lower_bounds, hgrn_ng, state_hgrn, hgrn_s, l, n_p)
        yc, c_p = _hgrn_prompt(zc, yc_s, c_lower_bounds, hgrn_ng, l, batch, seq)

        hb, res = _out_proj(ya, yb, yc, h, w_out_b, ln1, p_all, w_pg_b, w_pp_b, l, alpha)
        h, hb = _ffn(hb, res, w_gate_b, w_up_b, w_down_b, ln2, l)

        rwkv_p.append(r_p)
        shift_p.append(sh_p.reshape(batch, pb))
        hgrn_p.append(c_p)
        shift_s.append(sh_s)
        sgu_v.append(v_rows.reshape(n_s, 1, wa))

    return (h[:n_p].reshape(batch, seq, d), h[n_p:].reshape(n_s, 1, d), jnp.stack(rwkv_p), jnp.stack(shift_p),
            jnp.stack(hgrn_p), jnp.transpose(rwkv_s, (0, 4, 1, 2, 3)), jnp.stack(shift_s), hgrn_s, jnp.stack(sgu_v))
```

```python
import functools

import jax
import jax.numpy as jnp
from jax import lax
from jax.experimental import pallas as pl
from jax.experimental.pallas import tpu as pltpu

F32 = jnp.float32
BF16 = jnp.bfloat16

A_GROUPS = 4
A_CHUNK = 128
B_HEAD = 64
LORA_W, LORA_A, LORA_G = 64, 64, 128
C_HEAD = 128
C_CHUNK = 128
LN_EPS = 1e-5
B_GN_EPS = 1e-5 * B_HEAD
RMS_EPS = 1e-6

RWKV_CHUNK = 64
RWKV_STEP_ROWS = 128
HGRN_SUB = 16
SAMPLE_ROWS = 16
ROW_BLOCK = 128
LANES = 128
SUBLANES = 8
MXU_TILE = 256
PROJ_TILE_N = 512
VMEM_LIMIT_BYTES = 56 * 1024 * 1024


def _cparams(*sem):
    return pltpu.CompilerParams(dimension_semantics=sem, vmem_limit_bytes=VMEM_LIMIT_BYTES)


def _pick_tile(n, target, align):
    best = None
    for t in range(align, min(n, target) + 1, align):
        if n % t == 0:
            best = t
    assert best is not None, (n, target, align)
    return best


def _dot(a, b):
    return jnp.dot(a, b, preferred_element_type=F32)


def _dot_nt(a, b):
    return lax.dot_general(a, b, (((1,), (1,)), ((), ())), preferred_element_type=F32)


def _dot_tn(a, b):
    return lax.dot_general(a, b, (((0,), (0,)), ((), ())), preferred_element_type=F32)


def _bdot(a, b):
    return jnp.dot(a.astype(BF16), b.astype(BF16), preferred_element_type=F32)


def _bdot_nt(a, b):
    return _dot_nt(a.astype(BF16), b.astype(BF16))


def _layer_norm(x, g, b, eps):
    mu = jnp.mean(x, axis=-1, keepdims=True)
    xc = x - mu
    var = jnp.mean(xc * xc, axis=-1, keepdims=True)
    return xc * lax.rsqrt(var + eps) * g + b


def _gelu(x):
    return 0.5 * x * (1.0 + lax.erf(x * 0.7071067811865476))


def _sigmoid(x):
    return 1.0 / (1.0 + jnp.exp(-x))


def _log_sigmoid(x):
    return jnp.minimum(x, 0.0) - jnp.log1p(jnp.exp(-jnp.abs(x)))


def _split_bf16(x):
    hi = x.astype(BF16).astype(F32)
    return hi, x - hi


def _seg_sum(x, ones_bd, two_term=True):
    rows, width = x.shape
    t = ones_bd.shape[0]
    nb = width // t
    hi = x.astype(BF16)
    terms = (hi, (x - hi.astype(F32)).astype(BF16)) if two_term else (hi,)
    parts = [p[:, j * t:(j + 1) * t] for p in terms for j in range(nb)]
    r = _dot(jnp.concatenate(parts, axis=0), ones_bd)
    cols = []
    for j in range(nb):
        c = r[j * rows:(j + 1) * rows]
        if two_term:
            c = c + r[(nb + j) * rows:(nb + j + 1) * rows]
        cols.append(c)
    return jnp.concatenate(cols, axis=1)


def _cumsum_rows(x, period=None):
    rows, width = x.shape
    hi = x.astype(BF16)
    r1 = x - hi.astype(F32)
    mid = r1.astype(BF16)
    lo = (r1 - mid.astype(F32)).astype(BF16)
    ri = lax.broadcasted_iota(jnp.int32, (rows, rows), 0)
    ci = lax.broadcasted_iota(jnp.int32, (rows, rows), 1)
    tri = ri >= ci
    if period is not None and period < rows:
        tri = tri & (ci >= (ri // period) * period)
    c = _dot(jnp.where(tri, 1.0, 0.0).astype(BF16), jnp.concatenate([hi, mid, lo], axis=1))
    return c[:, :width] + c[:, width:2 * width] + c[:, 2 * width:]


def _layer_block(arr, layer):
    nd = arr.ndim - 1
    return pl.BlockSpec((None,) + arr.shape[1:], lambda *_: (layer,) + (0,) * nd)


def _whole(arr):
    nd = arr.ndim
    return pl.BlockSpec(arr.shape, lambda *_: (0,) * nd)


def _ln_in_kernel(xp_ref, xs_ref, g_ref, b_ref, h_ref, hb_ref, *, n_prompt_blocks):
    i = pl.program_id(0)

    def emit(x):
        h = _layer_norm(x, g_ref[...], b_ref[...], LN_EPS)
        h_ref[...] = h
        hb_ref[...] = h.astype(BF16)

    @pl.when(i < n_prompt_blocks)
    def _():
        emit(xp_ref[...])

    @pl.when(i >= n_prompt_blocks)
    def _():
        emit(xs_ref[...])


def _ln_in(xp, xs, g, b):
    n_p, d = xp.shape
    n_s = xs.shape[0]
    npb, nsb = n_p // ROW_BLOCK, n_s // ROW_BLOCK
    m = n_p + n_s
    return pl.pallas_call(
        functools.partial(_ln_in_kernel, n_prompt_blocks=npb),
        grid=(npb + nsb,),
        in_specs=[
            pl.BlockSpec((ROW_BLOCK, d), lambda i: (jnp.minimum(i, npb - 1), 0)),
            pl.BlockSpec((ROW_BLOCK, d), lambda i: (jnp.maximum(i - npb, 0), 0)),
            pl.BlockSpec((1, d), lambda i: (0, 0)),
            pl.BlockSpec((1, d), lambda i: (0, 0)),
        ],
        out_specs=[pl.BlockSpec((ROW_BLOCK, d), lambda i: (i, 0)), pl.BlockSpec((ROW_BLOCK, d), lambda i: (i, 0))],
        out_shape=[jax.ShapeDtypeStruct((m, d), F32), jax.ShapeDtypeStruct((m, d), BF16)],
        compiler_params=_cparams("parallel"),
        name="ln_in",
    )(xp, xs, g, b)


def _mm_kernel(x_ref, w_ref, o_ref, wb_scr):
    @pl.when(pl.program_id(1) == 0)
    def _():
        wb_scr[...] = w_ref[...].astype(BF16)

    o_ref[...] = jnp.dot(x_ref[...], wb_scr[...], preferred_element_type=F32)


def _mm(xb, w, layer, col0, n, name):
    m, k = xb.shape
    tm = _pick_tile(m, 1664, 16)
    tn = PROJ_TILE_N
    assert col0 % tn == 0 and n % tn == 0
    j0 = col0 // tn
    return pl.pallas_call(
        _mm_kernel,
        grid=(n // tn, m // tm),
        in_specs=[pl.BlockSpec((tm, k), lambda j, i: (i, 0)),
                  pl.BlockSpec((None, k, tn), lambda j, i: (layer, 0, j0 + j))],
        out_specs=pl.BlockSpec((tm, tn), lambda j, i: (i, j)),
        out_shape=jax.ShapeDtypeStruct((m, n), F32),
        scratch_shapes=[pltpu.VMEM((k, tn), BF16)],
        compiler_params=_cparams("parallel", "arbitrary"),
        name=name,
    )(xb, w)


def _sgu_kernel(u_ref, v_ref, ln_ref, ws_ref, bst_ref, y_ref, vn_ref, *, n_prompt_blocks):
    i = pl.program_id(0)
    gd = u_ref.shape[1] // A_GROUPS
    u = _gelu(u_ref[...])
    v = _gelu(v_ref[...])
    vn = [
        _layer_norm(v[:, g * gd:(g + 1) * gd], ln_ref[0:1, g * gd:(g + 1) * gd], ln_ref[1:2, g * gd:(g + 1) * gd], LN_EPS)
        for g in range(A_GROUPS)
    ]

    @pl.when(i < n_prompt_blocks)
    def _():
        row = lax.broadcasted_iota(jnp.int32, (A_CHUNK, A_CHUNK), 0)
        col = lax.broadcasted_iota(jnp.int32, (A_CHUNK, A_CHUNK), 1)
        for g in range(A_GROUPS):
            w_causal = jnp.where(row >= col, ws_ref[g], 0.0)
            mixed = _bdot(w_causal, vn[g]) + bst_ref[:, g:g + 1]
            y_ref[:, g * gd:(g + 1) * gd] = (u[:, g * gd:(g + 1) * gd] * mixed).astype(BF16)

    @pl.when(i >= n_prompt_blocks)
    def _():
        for g in range(A_GROUPS):
            mixed = vn[g] * ws_ref[g, 0:1, 0:1] + bst_ref[0:1, g:g + 1]
            y_ref[:, g * gd:(g + 1) * gd] = (u[:, g * gd:(g + 1) * gd] * mixed).astype(BF16)
            vn_ref[:, g * gd:(g + 1) * gd] = vn[g]


def _sgu(za, ln, ws, bst, layer, n_prompt):
    m = za.shape[0]
    wa = za.shape[1] // 2
    npb = n_prompt // A_CHUNK
    nb = m // A_CHUNK
    n_s = m - n_prompt
    return pl.pallas_call(
        functools.partial(_sgu_kernel, n_prompt_blocks=npb),
        grid=(nb,),
        in_specs=[
            pl.BlockSpec((A_CHUNK, wa), lambda i: (i, 0)),
            pl.BlockSpec((A_CHUNK, wa), lambda i: (i, 1)),
            _layer_block(ln, layer), _layer_block(ws, layer), _layer_block(bst, layer),
        ],
        out_specs=[
            pl.BlockSpec((A_CHUNK, wa), lambda i: (i, 0)),
            pl.BlockSpec((A_CHUNK, wa), lambda i: (jnp.maximum(i - npb, 0), 0)),
        ],
        out_shape=[jax.ShapeDtypeStruct((m, wa), BF16), jax.ShapeDtypeStruct((n_s, wa), F32)],
        compiler_params=_cparams("arbitrary"),
        name="sgu",
    )(za, za, ln, ws, bst)


_V_W0, _V_A0, _V_KK, _V_KA, _V_RK, _V_GNG, _V_GNB = range(7)


def _rwkv_prep(zb, prev, mu, vec, wup, aup, gup, ones_bd):
    wb = vec.shape[1]
    row = lambda j: vec[j:j + 1, :]
    xs = zb + mu * (prev - zb)
    r, k, v = xs[:, :wb], xs[:, wb:2 * wb], xs[:, 2 * wb:3 * wb]
    o4, o5 = 3 * wb + LORA_W, 3 * wb + LORA_W + LORA_A
    wd, ad, gd = xs[:, 3 * wb:o4], xs[:, o4:o5], xs[:, o5:]
    w = row(_V_W0) + _bdot(jnp.tanh(wd), wup)
    softplus_neg_w = jnp.maximum(-w, 0.0) + jnp.log1p(jnp.exp(-jnp.abs(w)))
    log_decay = -jnp.exp(-softplus_neg_w - 0.5)
    a = _sigmoid(row(_V_A0) + _bdot(ad, aup))
    g = _bdot(_sigmoid(gd), gup)
    kk = k * row(_V_KK)
    kk = kk / jnp.maximum(jnp.sqrt(_seg_sum(kk * kk, ones_bd)), 1e-12)
    kd = k * (1.0 + (a - 1.0) * row(_V_KA))
    bonus = _seg_sum(r * kd * row(_V_RK), ones_bd) * v
    return r, log_decay, kd, v, kk, a, g, bonus


def _rwkv_finish(y, bonus, g, vec, ones_bd):
    mean = _seg_sum(y, ones_bd, two_term=False) * (1.0 / B_HEAD)
    yc = y - mean
    var = _seg_sum(yc * yc, ones_bd, two_term=False) * (1.0 / B_HEAD)
    return (yc * lax.rsqrt(var + B_GN_EPS) * vec[_V_GNG:_V_GNG + 1, :] + vec[_V_GNB:_V_GNB + 1, :] + bonus) * g


_FAC_NAMES = ("rp", "kkp", "kp", "bp", "rhat", "kkhat", "ktil", "btil", "v")


def _rwkv_prompt_kernel(zb0_ref, zbn_ref, ys_ref, mu_ref, vec_ref, wup_ref, aup_ref, gup_ref, ones_ref,
                        y_ref, sout_ref, shift_ref, state_scr, prev_scr, fac_scr, gb_scr, ec_scr,
                        *, steps_per_seq, n_prompt_steps):
    i = pl.program_id(0)

    @pl.when(i >= n_prompt_steps)
    def _():
        y_ref[...] = ys_ref[...]

    @pl.when(i < n_prompt_steps)
    def _():
        s_idx = lax.rem(i, steps_per_seq)

        @pl.when(s_idx == 0)
        def _():
            state_scr[...] = jnp.zeros_like(state_scr)

        R = zbn_ref.shape[0]
        C = RWKV_CHUNK
        n_pairs = state_scr.shape[0]
        N = B_HEAD
        P = 2 * N
        ones_bd = ones_ref[...]
        vec = vec_ref[...]
        row1 = lax.broadcasted_iota(jnp.int32, (R, 1), 0)

        def prep_pieces(zb, prev_row):
            wb = vec.shape[1]
            vrow = lambda j: vec[j:j + 1, :]
            prev = jnp.where(row1 == 0, prev_row, pltpu.roll(zb, 1, 0))
            xs = zb + mu_ref[...] * (prev - zb)
            r, k, v = xs[:, :wb], xs[:, wb:2 * wb], xs[:, 2 * wb:3 * wb]
            o4, o5 = 3 * wb + LORA_W, 3 * wb + LORA_W + LORA_A
            wd, ad, gd = xs[:, 3 * wb:o4], xs[:, o4:o5], xs[:, o5:]
            yield
            w = vrow(_V_W0) + _bdot(jnp.tanh(wd), wup_ref[...])
            softplus_neg_w = jnp.maximum(-w, 0.0) + jnp.log1p(jnp.exp(-jnp.abs(w)))
            lw = -jnp.exp(-softplus_neg_w - 0.5)
            yield
            a = _sigmoid(vrow(_V_A0) + _bdot(ad, aup_ref[...]))
            g = _bdot(_sigmoid(gd), gup_ref[...])
            yield
            kk = k * vrow(_V_KK)
            kk = kk / jnp.maximum(jnp.sqrt(_seg_sum(kk * kk, ones_bd)), 1e-12)
            yield
            kd = k * (1.0 + (a - 1.0) * vrow(_V_KA))
            bonus = _seg_sum(r * kd * vrow(_V_RK), ones_bd) * v
            yield
            cum = _cumsum_rows(lw, C)
            yield
            facs = []
            for c in range(R // C):
                facs.append(chunk_factors(c, r, lw, kd, v, kk, a, cum))
                yield
            for j, name in enumerate(_FAC_NAMES):
                fac_scr[j] = jnp.concatenate([f[name] for f in facs], axis=0)
            for c, f in enumerate(facs):
                ec_scr[c:c + 1, :] = f["e_cend"]
            gb_scr[0] = g
            gb_scr[1] = bonus
            prev_scr[...] = zb[R - 1:R, :]

        lane = lax.broadcasted_iota(jnp.int32, (C, P), 1)
        rowc = lax.broadcasted_iota(jnp.int32, (C, P), 0)
        lo = lane < N
        col_in = jnp.where(lo, lane, lane - N)
        incl2 = rowc >= col_in
        strict2 = rowc > col_in
        lo2 = lax.broadcasted_iota(jnp.int32, (2 * C, P), 1) < N
        bd_mask = (lax.broadcasted_iota(jnp.int32, (P, P), 0) < N) == (lax.broadcasted_iota(jnp.int32, (P, P), 1) < N)
        pairs = [slice(p * P, (p + 1) * P) for p in range(n_pairs)]

        def chunk_factors(c, r_all, lw_all, kd_all, v_all, kk_all, a_all, cum_all):
            rs = slice(c * C, (c + 1) * C)
            r, lw, kd, v, kk, a, cum = (t[rs] for t in (r_all, lw_all, kd_all, v_all, kk_all, a_all, cum_all))
            m = cum[C // 2 - 1:C // 2, :]
            cend = cum[C - 1:C, :]
            e_pos = jnp.exp(cum - m)
            e_neg = jnp.exp(m - cum)
            e_prev = jnp.exp(cum - lw - m)
            e_m = jnp.exp(m)
            e_end = jnp.exp(cend - m)
            rp = r * e_pos
            kkp = kk * e_prev
            kp = kd * e_neg
            bp = kk * a * e_neg
            rhat = rp * e_m
            kkhat = kkp * e_m
            ktil = kp * e_end
            btil = bp * e_end
            e_cend = e_end * e_m
            return dict(rp=rp, kkp=kkp, kp=kp, bp=bp, rhat=rhat, kkhat=kkhat, ktil=ktil, btil=btil, v=v,
                        e_cend=e_cend)

        def local_units(facs, between_stages):
            units = [(f, ps) for f in facs for ps in pairs]
            scores = []
            for f, ps in units:
                q_pair = jnp.concatenate([f["rp"][:, ps], f["kkp"][:, ps]], axis=0)
                q1 = jnp.where(lo2, q_pair, 0.0).astype(BF16)
                q2 = jnp.where(lo2, 0.0, q_pair).astype(BF16)
                kp_b, bp_b = f["kp"][:, ps].astype(BF16), f["bp"][:, ps].astype(BF16)
                scores.append((_dot_nt(q1, jnp.concatenate([bp_b, kp_b], axis=0)),
                               _dot_nt(q2, jnp.concatenate([kp_b, bp_b], axis=0))))
            between_stages()
            ar1, ar2, n_bd, avs, v_b = [], [], [], [], []
            for (f, ps), (sc1, sc2) in zip(units, scores):
                ar1.append(jnp.where(incl2, sc1[:C], 0.0).astype(BF16))
                ar2.append(jnp.where(incl2, sc2[:C], 0.0).astype(BF16))
                kn1 = jnp.where(strict2, sc1[C:], 0.0)
                kn2 = jnp.where(strict2, sc2[C:], 0.0)
                n_bd.append(jnp.concatenate([jnp.where(lo, kn1, 0.0), jnp.where(lo, 0.0, kn2)], axis=0).astype(BF16))
                akk = jnp.concatenate([jnp.where(lo, 0.0, kn1), jnp.where(lo, kn2, 0.0)], axis=0).astype(BF16)
                v_pair = f["v"][:, ps]
                v_b.append(v_pair.astype(BF16))
                v_sw = pltpu.roll(v_pair, N, 1).astype(BF16)
                avs.append(_dot(akk, jnp.concatenate([v_sw, v_sw], axis=0)))
            between_stages()
            xs = []
            for (f, ps), av in zip(units, avs):
                kkh = f["kkhat"][:, ps]
                xs.append(jnp.concatenate([jnp.where(lo, kkh, av[:C]), jnp.where(lo, av[C:], kkh)], axis=0))
            pw = n_bd
            span = 1
            while span < C:
                last = 2 * span >= C
                rhs = [x.astype(BF16) if last else jnp.concatenate([t, x.astype(BF16)], axis=1)
                       for t, x in zip(pw, xs)]
                prod = [_dot(t, r) for t, r in zip(pw, rhs)]
                px = prod if last else [r[:, P:] for r in prod]
                xs = [x - d for x, d in zip(xs, px)] if span == 1 else [x + d for x, d in zip(xs, px)]
                if not last:
                    pw = [r[:, :P].astype(BF16) for r in prod]
                between_stages()
                span *= 2
            zeros_cp = jnp.zeros((C, P), BF16)
            zs, w_parts = [], []
            for (f, ps), x, a1, a2, vb in zip(units, xs, ar1, ar2, v_b):
                w1 = jnp.where(lo, x[:C], x[C:])
                w2 = pltpu.roll(jnp.where(lo, x[C:], x[:C]), N, 1)
                w1_hi, w1_lo = _split_bf16(w1)
                w2_hi, w2_lo = _split_bf16(w2)
                nw = jnp.concatenate([-w1_hi, -w2_hi], axis=1).astype(BF16)
                zv = jnp.concatenate([zeros_cp, vb], axis=1)
                zs.append((_dot(a1, jnp.concatenate([nw, zv], axis=0)),
                           _dot(a2, jnp.concatenate([zv, nw], axis=0))))
                w_parts.append((w1_hi, w1_lo, w2_hi, w2_lo))
            out = []
            for (f, ps), (z1, z2), (w1_hi, w1_lo, w2_hi, w2_lo), vb in zip(units, zs, w_parts, v_b):
                qeff = (f["rhat"][:, ps] + jnp.where(lo, z1[:, :P], z2[:, :P])).astype(BF16)
                yloc = jnp.where(lo, z1[:, P:], z2[:, P:])
                kt_b, bt_b = f["ktil"][:, ps].astype(BF16), f["btil"][:, ps].astype(BF16)
                w1tb = _dot_tn(jnp.concatenate([w1_hi, w1_lo], axis=0).astype(BF16),
                               jnp.concatenate([bt_b, bt_b], axis=0))
                hm = _dot_tn(jnp.concatenate([vb, (-w2_hi).astype(BF16), (-w2_lo).astype(BF16)], axis=0),
                             jnp.concatenate([kt_b, bt_b, bt_b], axis=0))
                out.append(dict(qeff=qeff, yloc=yloc, g_bd=jnp.where(bd_mask, -w1tb, 0.0).astype(BF16),
                                h_bd=jnp.where(bd_mask, hm, 0.0), e_cend=f["e_cend"][:, ps]))
            return [out[c * n_pairs:(c + 1) * n_pairs] for c in range(len(facs))]

        def state_part(loc, states):
            ys, new_states = [], []
            for u, s_bd in zip(loc, states):
                s_hi, s_lo = _split_bf16(s_bd)
                s_hi_b = s_hi.astype(BF16)
                ys.append(_dot_nt(u["qeff"], s_hi_b) + u["yloc"])
                g2 = jnp.concatenate([u["g_bd"], u["g_bd"]], axis=0)
                sg = _dot(jnp.concatenate([s_hi_b, s_lo.astype(BF16)], axis=1), g2)
                new_states.append(s_bd * u["e_cend"] + sg + u["h_bd"])
            return jnp.concatenate(ys, axis=1), new_states

        @pl.when(i == 0)
        def _():
            for _ in prep_pieces(zb0_ref[...], jnp.zeros_like(prev_scr)):
                pass

        last_row = prev_scr[...]
        facs = []
        for c in range(R // C):
            f = {name: fac_scr[j, c * C:(c + 1) * C, :] for j, name in enumerate(_FAC_NAMES)}
            f["e_cend"] = ec_scr[c:c + 1, :]
            facs.append(f)
        g, bonus = gb_scr[0], gb_scr[1]

        next_starts_seq = lax.rem(i + 1, steps_per_seq) == 0
        next_prep = prep_pieces(zbn_ref[...], jnp.where(next_starts_seq, 0.0, last_row))
        locs = local_units(facs, lambda: next(next_prep, None))
        states = [state_scr[p] for p in range(n_pairs)]
        y_rows = []
        for loc in locs:
            y_c, states = state_part(loc, states)
            y_rows.append(y_c)
        for p in range(n_pairs):
            state_scr[p] = states[p]

        y = jnp.concatenate(y_rows, axis=0)
        y_ref[...] = _rwkv_finish(y, bonus, g, vec, ones_bd).astype(BF16)
        for _ in next_prep:
            pass

        @pl.when(s_idx == steps_per_seq - 1)
        def _():
            shift_ref[0] = last_row
            for p in range(n_pairs):
                sout_ref[0, 2 * p] = states[p][:N, :N]
                sout_ref[0, 2 * p + 1] = states[p][N:, N:]


def _rwkv_prompt(zb, y_sample, params, layer, batch, seq):
    mu, vec, wup, aup, gup, ones_bd = params
    pb = zb.shape[1]
    wb = vec.shape[2]
    n_heads = wb // B_HEAD
    R = RWKV_STEP_ROWS
    sps = seq // R
    n_steps = batch * sps
    n_s = y_sample.shape[0]
    m = batch * seq + n_s
    seq_of = lambda i: jnp.minimum(i // sps, batch - 1)
    return pl.pallas_call(
        functools.partial(_rwkv_prompt_kernel, steps_per_seq=sps, n_prompt_steps=n_steps),
        grid=(n_steps + n_s // R,),
        in_specs=[
            pl.BlockSpec((R, pb), lambda i: (0, 0), pipeline_mode=pl.Buffered(1)),
            pl.BlockSpec((R, pb), lambda i: (jnp.minimum(i + 1, n_steps - 1), 0)),
            pl.BlockSpec((R, wb), lambda i: (jnp.maximum(i - n_steps, 0), 0)),
            _layer_block(mu, layer), _layer_block(vec, layer), _layer_block(wup, layer),
            _layer_block(aup, layer), _layer_block(gup, layer), _whole(ones_bd),
        ],
        out_specs=[
            pl.BlockSpec((R, wb), lambda i: (i, 0)),
            pl.BlockSpec((1, n_heads, B_HEAD, B_HEAD), lambda i: (seq_of(i), 0, 0, 0)),
            pl.BlockSpec((1, 1, pb), lambda i: (seq_of(i), 0, 0)),
        ],
        out_shape=[
            jax.ShapeDtypeStruct((m, wb), BF16),
            jax.ShapeDtypeStruct((batch, n_heads, B_HEAD, B_HEAD), F32),
            jax.ShapeDtypeStruct((batch, 1, pb), F32),
        ],
        scratch_shapes=[pltpu.VMEM((n_heads // 2, 2 * B_HEAD, 2 * B_HEAD), F32), pltpu.VMEM((1, pb), F32),
                        pltpu.VMEM((len(_FAC_NAMES), R, wb), F32), pltpu.VMEM((2, R, wb), F32),
                        pltpu.VMEM((SUBLANES, wb), F32)],
        compiler_params=_cparams("arbitrary"),
        name="rwkv_prompt",
    )(zb, zb, y_sample, mu, vec, wup, aup, gup, ones_bd)


def _rwkv_sample_kernel(zb_ref, shift_ref, mu_ref, vec_ref, wup_ref, aup_ref, gup_ref, ones_ref, s_ref, buf_ref,
                        y_ref, sout_ref, shift_out_ref,
                        kk_t, wr_t, b_t, kd_t, v_t, w_t, r_t, y_t, g_scr, bonus_scr):
    del buf_ref
    p = pl.program_id(1)
    heads_per_step = s_ref.shape[0]
    N = B_HEAD

    @pl.when(p == 0)
    def _():
        zb = zb_ref[...]
        shift_out_ref[...] = zb
        r, lw, kd, v, kk, a, g, bonus = _rwkv_prep(
            zb, shift_ref[...], mu_ref[...], vec_ref[...], wup_ref[...], aup_ref[...], gup_ref[...], ones_ref[...])
        w = jnp.exp(lw)
        kk_t[...] = kk.T
        wr_t[...] = (w * r).T
        b_t[...] = (kk * a).T
        kd_t[...] = kd.T
        v_t[...] = v.T
        w_t[...] = w.T
        r_t[...] = r.T
        g_scr[...] = g
        bonus_scr[...] = bonus

    for hh in range(heads_per_step):
        base = pl.multiple_of((p * heads_per_step + hh) * N, N)
        hs = pl.ds(base, N)
        kk_h, wr_h, b_h, kd_h, w_h, r_h = kk_t[hs, :], wr_t[hs, :], b_t[hs, :], kd_t[hs, :], w_t[hs, :], r_t[hs, :]
        b_dot_r = jnp.sum(b_h * r_h, axis=0, keepdims=True)
        k_dot_r = jnp.sum(kd_h * r_h, axis=0, keepdims=True)

        def body(vi, carry):
            s0 = s_ref[hh, vi]
            s_kk = jnp.sum(s0 * kk_h, axis=0, keepdims=True)
            yq = jnp.sum(s0 * wr_h, axis=0, keepdims=True)
            v_row = v_t[pl.ds(base + vi, 1), :]
            sout_ref[hh, vi] = s0 * w_h - s_kk * b_h + v_row * kd_h
            y_t[pl.ds(base + vi, 1), :] = yq - s_kk * b_dot_r + v_row * k_dot_r
            return carry

        lax.fori_loop(0, N, body, 0, unroll=4)

    @pl.when(p == pl.num_programs(1) - 1)
    def _():
        y = y_t[...].T
        y_ref[...] = _rwkv_finish(y, bonus_scr[...], g_scr[...], vec_ref[...], ones_ref[...]).astype(BF16)


def _rwkv_sample(zb, shift, state_t, out_buf, params, layer, n_prompt):
    mu, vec, wup, aup, gup, ones_bd = params
    pb = zb.shape[1]
    wb = vec.shape[2]
    n_heads, n_s = state_t.shape[1], state_t.shape[4]
    hps = 2
    off = n_prompt // LANES
    sspec = pl.BlockSpec((None, hps, B_HEAD, B_HEAD, LANES), lambda sb, p: (layer, p, 0, 0, sb))
    return pl.pallas_call(
        _rwkv_sample_kernel,
        grid=(n_s // LANES, n_heads // hps),
        in_specs=[
            pl.BlockSpec((LANES, pb), lambda sb, p: (off + sb, 0)),
            pl.BlockSpec((None, LANES, pb), lambda sb, p: (layer, sb, 0)),
            _layer_block(mu, layer), _layer_block(vec, layer), _layer_block(wup, layer),
            _layer_block(aup, layer), _layer_block(gup, layer), _whole(ones_bd),
            sspec, pl.BlockSpec(memory_space=pl.ANY),
        ],
        out_specs=[pl.BlockSpec((LANES, wb), lambda sb, p: (sb, 0)), sspec,
                   pl.BlockSpec((LANES, pb), lambda sb, p: (sb, 0))],
        out_shape=[jax.ShapeDtypeStruct((n_s, wb), BF16), jax.ShapeDtypeStruct(state_t.shape, F32),
                   jax.ShapeDtypeStruct((n_s, pb), F32)],
        input_output_aliases={9: 1},
        scratch_shapes=[pltpu.VMEM((wb, LANES), F32)] * 8 + [pltpu.VMEM((LANES, wb), F32)] * 2,
        compiler_params=_cparams("arbitrary", "arbitrary"),
        name="rwkv_sample",
    )(zb, shift, mu, vec, wup, aup, gup, ones_bd, state_t, out_buf)


def _hgrn_gates(fz, clb, layer):
    tail = jnp.log1p(jnp.exp(-jnp.abs(fz)))
    ls_pos = jnp.minimum(fz, 0.0) - tail
    ls_neg = jnp.minimum(-fz, 0.0) - tail
    sig_neg = jnp.exp(ls_neg)
    if layer == 0:
        return ls_pos, sig_neg
    e = jnp.exp(clb - jnp.max(clb, axis=0, keepdims=True))
    sm = e / jnp.sum(e, axis=0, keepdims=True)
    lb = jnp.sum(sm[1:layer + 1], axis=0, keepdims=True)
    x2 = jnp.log(lb) + ls_neg
    log_f = jnp.maximum(ls_pos, x2) + jnp.log1p(jnp.exp(-jnp.abs(ls_pos - x2)))
    return log_f, (1.0 - lb) * sig_neg


def _hgrn_finish(o, g, norm_g):
    on = o * lax.rsqrt(jnp.mean(o * o, axis=-1, keepdims=True) + RMS_EPS) * norm_g
    return on * (g * _sigmoid(g))


def _hgrn_prompt_kernel(zc_ref, ys_ref, clb_ref, ng_ref, y_ref, sout_ref, state_scr, *, layer, n_chunks, n_prompt_steps):
    i = pl.program_id(0)

    @pl.when(i >= n_prompt_steps)
    def _():
        y_ref[...] = ys_ref[...]

    @pl.when(i < n_prompt_steps)
    def _():
        c_idx = lax.rem(i, n_chunks)

        @pl.when(c_idx == 0)
        def _():
            state_scr[...] = jnp.zeros_like(state_scr)

        z = zc_ref[...]
        C = z.shape[0]
        wc = z.shape[1] // 4
        n_heads = wc // C_HEAD
        D = C_HEAD
        SUB = HGRN_SUB
        HALF = SUBLANES
        nsub = C // SUB
        q, fz, iv, g = z[:, :wc], z[:, wc:2 * wc], z[:, 2 * wc:3 * wc], z[:, 3 * wc:]
        log_f, kg = _hgrn_gates(fz, clb_ref[...], layer)
        bcum = _cumsum_rows(log_f)
        e_b = jnp.exp(bcum)
        b_end = bcum[C - 1:C, :]
        e_end = jnp.exp(b_end)
        k_hat = kg * jnp.exp(b_end - bcum)
        t_full = lax.broadcasted_iota(jnp.int32, (nsub, SUB, 1), 1)
        t_half = lax.broadcasted_iota(jnp.int32, (nsub, SUB - HALF, 1), 1) + HALF
        level_sizes = [SUB << li for li in range((C // SUB).bit_length() - 1)]
        half = C // 2
        n_lv = len(level_sizes) * half
        ri = lax.broadcasted_iota(jnp.int32, (n_lv, n_lv), 0)
        ci = lax.broadcasted_iota(jnp.int32, (n_lv, n_lv), 1)
        level_mask = None
        for li, s in enumerate(level_sizes):
            shift = s.bit_length() - 1
            in_level = (ri >= li * half) & (ri < (li + 1) * half) & (ci >= li * half) & (ci < (li + 1) * half)
            same_block = ((ri - li * half) >> shift) == ((ci - li * half) >> shift)
            lm = in_level & same_block
            level_mask = lm if level_mask is None else (level_mask | lm)

        outs = []
        for h in range(n_heads):
            sl = slice(h * D, (h + 1) * D)
            qh, kh, vh, bh = q[:, sl], kg[:, sl], iv[:, sl], bcum[:, sl]
            st = state_scr[h]
            o = _bdot_nt(qh * e_b[:, sl], st)
            q3, k3, v3, b3 = (t.reshape(nsub, SUB, D) for t in (qh, kh, vh, bh))
            q3h, b3h = q3[:, HALF:, :], b3[:, HALF:, :]
            od = jnp.zeros((nsub, SUB, D), F32)
            odh = jnp.zeros((nsub, SUB - HALF, D), F32)
            for j in range(SUB):
                if j < HALF:
                    dec = jnp.exp(b3 - b3[:, j:j + 1, :])
                    att = jnp.sum(q3 * k3[:, j:j + 1, :] * dec, axis=-1, keepdims=True)
                    od = od + jnp.where(t_full >= j, att, 0.0) * v3[:, j:j + 1, :]
                else:
                    dec = jnp.exp(b3h - b3[:, j:j + 1, :])
                    att = jnp.sum(q3h * k3[:, j:j + 1, :] * dec, axis=-1, keepdims=True)
                    odh = odh + jnp.where(t_half >= j, att, 0.0) * v3[:, j:j + 1, :]
            od = od + jnp.concatenate([jnp.zeros((nsub, HALF, D), F32), odh], axis=1)
            o = o + od.reshape(C, D)
            q_l, k_l, v_l = [], [], []
            for s in level_sizes:
                for j in range(C // (2 * s)):
                    a0 = 2 * j * s
                    b_bound = bh[a0 + s - 1:a0 + s, :]
                    q_l.append(qh[a0 + s:a0 + 2 * s, :] * jnp.exp(bh[a0 + s:a0 + 2 * s, :] - b_bound))
                    k_l.append(kh[a0:a0 + s, :] * jnp.exp(b_bound - bh[a0:a0 + s, :]))
                    v_l.append(vh[a0:a0 + s, :])
            att = _bdot_nt(jnp.concatenate(q_l, axis=0), jnp.concatenate(k_l, axis=0))
            o_lv = _bdot(jnp.where(level_mask, att, 0.0), jnp.concatenate(v_l, axis=0))
            for li, s in enumerate(level_sizes):
                pieces = []
                for j in range(C // (2 * s)):
                    r0 = li * (C // 2) + j * s
                    pieces += [jnp.zeros((s, D), F32), o_lv[r0:r0 + s, :]]
                o = o + jnp.concatenate(pieces, axis=0)
            outs.append(o)
            vh_hi, vh_lo = _split_bf16(vh)
            kh_hi, kh_lo = _split_bf16(k_hat[:, sl])
            upd = _dot_tn(jnp.concatenate([vh_hi, vh_hi, vh_lo], axis=0).astype(BF16),
                          jnp.concatenate([kh_hi, kh_lo, kh_hi], axis=0).astype(BF16))
            state_scr[h] = st * e_end[:, sl] + upd

        o_all = jnp.concatenate(outs, axis=1)
        y_ref[...] = _hgrn_finish(o_all, g, ng_ref[...]).astype(BF16)

        @pl.when(c_idx == n_chunks - 1)
        def _():
            for h in range(n_heads):
                sout_ref[0, h] = state_scr[h].T


def _hgrn_prompt(zc, y_sample, clb, norm_g, layer, batch, seq):
    pc = zc.shape[1]
    wc = pc // 4
    n_heads = wc // C_HEAD
    C = C_CHUNK
    nc = seq // C
    n_steps = batch * nc
    n_s = y_sample.shape[0]
    m = batch * seq + n_s
    return pl.pallas_call(
        functools.partial(_hgrn_prompt_kernel, layer=layer, n_chunks=nc, n_prompt_steps=n_steps),
        grid=(n_steps + n_s // C,),
        in_specs=[
            pl.BlockSpec((C, pc), lambda i: (jnp.minimum(i, n_steps - 1), 0)),
            pl.BlockSpec((C, wc), lambda i: (jnp.maximum(i - n_steps, 0), 0)),
            _whole(clb), _layer_block(norm_g, layer),
        ],
        out_specs=[
            pl.BlockSpec((C, wc), lambda i: (i, 0)),
            pl.BlockSpec((1, n_heads, C_HEAD, C_HEAD), lambda i: (jnp.minimum(i // nc, batch - 1), 0, 0, 0)),
        ],
        out_shape=[
            jax.ShapeDtypeStruct((m, wc), BF16),
            jax.ShapeDtypeStruct((batch, n_heads, C_HEAD, C_HEAD), F32),
        ],
        scratch_shapes=[pltpu.VMEM((n_heads, C_HEAD, C_HEAD), F32)],
        compiler_params=_cparams("arbitrary"),
        name="hgrn_prompt",
    )(zc, y_sample, clb, norm_g)


def _hgrn_sample_kernel(zc_ref, clb_ref, ng_ref, s_ref, buf_ref, y_ref, sout_ref,
                        q_scr, f_scr, k_scr, v_scr, o_scr, *, layer):
    del buf_ref
    z = zc_ref[...]
    n_rows, n_heads = s_ref.shape[0], s_ref.shape[1]
    wc = z.shape[1] // 4
    D = C_HEAD
    q, fz, iv, g = z[:, :wc], z[:, wc:2 * wc], z[:, 2 * wc:3 * wc], z[:, 3 * wc:]
    log_f, kg = _hgrn_gates(fz, clb_ref[...], layer)
    q_scr[...] = q
    f_scr[...] = jnp.exp(log_f)
    k_scr[...] = kg
    v_scr[...] = iv
    pad = jnp.zeros((SUBLANES - n_heads, D), F32)

    def body(i, carry):
        q_r, f_r, k_r, v_r = (t[pl.ds(i, 1), :] for t in (q_scr, f_scr, k_scr, v_scr))
        heads = lambda t: jnp.concatenate([t[:, h * D:(h + 1) * D] for h in range(n_heads)] + [pad], axis=0)
        f_cols = heads(f_r).T
        k_cols = heads(k_r).T
        os_ = []
        for h in range(n_heads):
            sl = slice(h * D, (h + 1) * D)
            s0 = s_ref[i, h]
            qk = jnp.sum(q_r[:, sl] * k_r[:, sl], axis=-1, keepdims=True)
            os_.append(_bdot(q_r[:, sl] * f_r[:, sl], s0) + qk * v_r[:, sl])
            sout_ref[i, h] = s0 * f_cols[:, h:h + 1] + k_cols[:, h:h + 1] * v_r[:, sl]
        o_scr[pl.ds(i, 1), :] = jnp.concatenate(os_, axis=1)
        return carry

    lax.fori_loop(0, n_rows, body, 0)
    y_ref[...] = _hgrn_finish(o_scr[...], g, ng_ref[...]).astype(BF16)


def _hgrn_sample(zc, clb, norm_g, state, out_buf, layer, n_prompt):
    pc = zc.shape[1]
    wc = pc // 4
    n_s, n_heads = state.shape[1], state.shape[2]
    R = SAMPLE_ROWS
    off = n_prompt // R
    sspec = pl.BlockSpec((None, R, n_heads, C_HEAD, C_HEAD), lambda i: (layer, i, 0, 0, 0))
    return pl.pallas_call(
        functools.partial(_hgrn_sample_kernel, layer=layer),
        grid=(n_s // R,),
        in_specs=[pl.BlockSpec((R, pc), lambda i: (off + i, 0)), _whole(clb), _layer_block(norm_g, layer),
                  sspec, pl.BlockSpec(memory_space=pl.ANY)],
        out_specs=[pl.BlockSpec((R, wc), lambda i: (i, 0)), sspec],
        out_shape=[jax.ShapeDtypeStruct((n_s, wc), BF16), jax.ShapeDtypeStruct(state.shape, F32)],
        input_output_aliases={4: 1},
        scratch_shapes=[pltpu.VMEM((R, wc), F32)] * 5,
        compiler_params=_cparams("parallel"),
        name="hgrn_sample",
    )(zc, clb, norm_g, state, out_buf)


def _out_kernel(ya_ref, yb_ref, yc_ref, h_ref, wo_ref, ln_ref, p_ref, wpg_ref, wpp_ref, hb_ref, res_ref, *, alpha):
    wa, wb = ya_ref.shape[1], yb_ref.shape[1]
    mix = (_dot(ya_ref[...], wo_ref[:wa, :]) + _dot(yb_ref[...], wo_ref[wa:wa + wb, :])
           + _dot(yc_ref[...], wo_ref[wa + wb:, :]))
    h1 = _layer_norm(alpha * h_ref[...] + mix, ln_ref[0:1, :], ln_ref[1:2, :], LN_EPS)
    h1b = h1.astype(BF16)
    ple = _sigmoid(_dot(h1b, wpg_ref[...])) * _dot(p_ref[...], wpp_ref[...])
    hb_ref[...] = h1b
    res_ref[...] = alpha * h1 + ple


def _out_proj(ya, yb, yc, h, wo, ln, p, wpg, wpp, layer, alpha):
    m, d = h.shape
    tm = _pick_tile(m, 320, 16)
    rows = lambda a: pl.BlockSpec((tm, a.shape[-1]), lambda i: (i, 0))
    const = lambda a: pl.BlockSpec((None,) + a.shape[1:], lambda i: (layer, 0, 0), pipeline_mode=pl.Buffered(1))
    return pl.pallas_call(
        functools.partial(_out_kernel, alpha=alpha),
        grid=(m // tm,),
        in_specs=[rows(ya), rows(yb), rows(yc), rows(h), const(wo), const(ln),
                  pl.BlockSpec((None, tm, p.shape[-1]), lambda i: (layer, i, 0)), const(wpg), const(wpp)],
        out_specs=[rows(h), rows(h)],
        out_shape=[jax.ShapeDtypeStruct((m, d), BF16), jax.ShapeDtypeStruct((m, d), F32)],
        compiler_params=_cparams("parallel"),
        name="out_proj",
    )(ya, yb, yc, h, wo, ln, p, wpg, wpp)


def _ffn_kernel(hb_ref, res_ref, wg_ref, wu_ref, wd_ref, ln_ref, h_ref, hbo_ref, acc_ref):
    f = pl.program_id(1)

    @pl.when(f == 0)
    def _():
        acc_ref[...] = jnp.zeros_like(acc_ref)

    x = hb_ref[...]
    gate = _dot(x, wg_ref[...])
    up = _dot(x, wu_ref[...])
    act = (gate * _sigmoid(gate) * up).astype(BF16)
    acc_ref[...] += _dot(act, wd_ref[...])

    @pl.when(f == pl.num_programs(1) - 1)
    def _():
        h2 = _layer_norm(res_ref[...] + acc_ref[...], ln_ref[0:1, :], ln_ref[1:2, :], LN_EPS)
        h_ref[...] = h2
        hbo_ref[...] = h2.astype(BF16)


def _ffn(hb, res, wg, wu, wd, ln, layer):
    m, d = hb.shape
    dff = wg.shape[2]
    tm = _pick_tile(m, 640, 16)
    tf = _pick_tile(dff, 512, 128)
    return pl.pallas_call(
        _ffn_kernel,
        grid=(m // tm, dff // tf),
        in_specs=[
            pl.BlockSpec((tm, d), lambda i, f: (i, 0)),
            pl.BlockSpec((tm, d), lambda i, f: (i, 0)),
            pl.BlockSpec((None, d, tf), lambda i, f: (layer, 0, f)),
            pl.BlockSpec((None, d, tf), lambda i, f: (layer, 0, f)),
            pl.BlockSpec((None, tf, d), lambda i, f: (layer, f, 0)),
            pl.BlockSpec((None, 2, d), lambda i, f: (layer, 0, 0)),
        ],
        out_specs=[pl.BlockSpec((tm, d), lambda i, f: (i, 0)), pl.BlockSpec((tm, d), lambda i, f: (i, 0))],
        out_shape=[jax.ShapeDtypeStruct((m, d), F32), jax.ShapeDtypeStruct((m, d), BF16)],
        scratch_shapes=[pltpu.VMEM((tm, d), F32)],
        compiler_params=_cparams("parallel", "arbitrary"),
        name="ffn",
    )(hb, res, wg, wu, wd, ln)


def kernel(x_prompt, x_sample, state_rwkv, state_shift, state_hgrn, p_prompt, p_sample, ln_in_g, ln_in_b, w_in, a_ln_g, a_ln_b, a_ws, a_bs, b_mu, b_w0, b_w_up, b_a0, b_a_up, b_g_up, b_k_k, b_k_a, b_r_k, b_gn_g, b_gn_b, c_lower_bounds, c_norm_g, w_out, ln1_g, ln1_b, w_ffn_gate, w_ffn_up, w_ffn_down, w_ple_gate, w_ple_proj, ln2_g, ln2_b):
    batch, seq, d = x_prompt.shape
    n_s = x_sample.shape[0]
    depth = w_in.shape[0]
    n_p = batch * seq
    wa, wb, wc = a_ln_g.shape[1], b_w0.shape[1], c_norm_g.shape[1]
    pa, pb = 2 * wa, 3 * wb + LORA_W + LORA_A + LORA_G
    alpha = float((2 * depth) ** 0.25)
    assert x_sample.shape[1] == 1 and seq % A_CHUNK == 0 and seq % C_CHUNK == 0 and seq % RWKV_STEP_ROWS == 0
    assert n_s % ROW_BLOCK == 0 and n_p % ROW_BLOCK == 0 and n_s % LANES == 0
    assert (wb // B_HEAD) % 2 == 0 and wb % MXU_TILE == 0

    bf = lambda t: t.astype(BF16)
    pc = w_in.shape[2] - pa - pb
    w_out_b, w_gate_b, w_up_b, w_down_b = bf(w_out), bf(w_ffn_gate), bf(w_ffn_up), bf(w_ffn_down)
    w_pg_b, w_pp_b = bf(w_ple_gate), bf(w_ple_proj)
    p_all = bf(jnp.concatenate([p_prompt.reshape(depth, n_p, -1), p_sample.reshape(depth, n_s, -1)], axis=1))
    sgu_ln = jnp.stack([a_ln_g, a_ln_b], axis=1)
    sgu_bst = jnp.swapaxes(a_bs, 1, 2)
    rwkv_vec = jnp.stack([b_w0, b_a0, b_k_k, b_k_a, b_r_k.reshape(depth, wb), b_gn_g, b_gn_b, jnp.zeros_like(b_w0)], axis=1)
    idx = jnp.arange(MXU_TILE)
    ones_bd = bf(idx[:, None] // B_HEAD == idx[None, :] // B_HEAD)
    rwkv_params = (b_mu.reshape(depth, 1, pb), rwkv_vec, bf(b_w_up), bf(b_a_up), bf(b_g_up), ones_bd)
    hgrn_ng = c_norm_g.reshape(depth, 1, wc)
    ln1 = jnp.stack([ln1_g, ln1_b], axis=1)
    ln2 = jnp.stack([ln2_g, ln2_b], axis=1)
    row = lambda t: t.reshape(1, -1)

    h, hb = _ln_in(x_prompt.reshape(n_p, d), x_sample.reshape(n_s, d), row(ln_in_g), row(ln_in_b))

    state_rwkv_t = jnp.transpose(state_rwkv, (0, 2, 3, 4, 1))
    rwkv_s = jnp.zeros(state_rwkv_t.shape, F32)
    hgrn_s = jnp.zeros(state_hgrn.shape, F32)
    rwkv_p, shift_p, hgrn_p, shift_s, sgu_v = [], [], [], [], []
    for l in range(depth):
        za = _mm(hb, w_in, l, 0, pa, "proj_a")
        zb = _mm(hb, w_in, l, pa, pb, "proj_b")
        zc = _mm(hb, w_in, l, pa + pb, pc, "proj_c")

        ya, v_rows = _sgu(za, sgu_ln, a_ws, sgu_bst, l, n_p)
        yb_s, rwkv_s, sh_s = _rwkv_sample(zb, state_shift, state_rwkv_t, rwkv_s, rwkv_params, l, n_p)
        yb, r_p, sh_p = _rwkv_prompt(zb, yb_s, rwkv_params, l, batch, seq)
        yc_s, hgrn_s = _hgrn_sample(zc, c_lower_bounds, hgrn_ng, state_hgrn, hgrn_s, l, n_p)
        yc, c_p = _hgrn_prompt(zc, yc_s, c_lower_bounds, hgrn_ng, l, batch, seq)

        hb, res = _out_proj(ya, yb, yc, h, w_out_b, ln1, p_all, w_pg_b, w_pp_b, l, alpha)
        h, hb = _ffn(hb, res, w_gate_b, w_up_b, w_down_b, ln2, l)

        rwkv_p.append(r_p)
        shift_p.append(sh_p.reshape(batch, pb))
        hgrn_p.append(c_p)
        shift_s.append(sh_s)
        sgu_v.append(v_rows.reshape(n_s, 1, wa))

    return (h[:n_p].reshape(batch, seq, d), h[n_p:].reshape(n_s, 1, d), jnp.stack(rwkv_p), jnp.stack(shift_p),
            jnp.stack(hgrn_p), jnp.transpose(rwkv_s, (0, 4, 1, 2, 3)), jnp.stack(shift_s), hgrn_s, jnp.stack(sgu_v))
```

```python
import functools

import jax
import jax.numpy as jnp
from jax import lax
from jax.experimental import pallas as pl
from jax.experimental.pallas import tpu as pltpu

F32 = jnp.float32
BF16 = jnp.bfloat16

A_GROUPS = 4
A_CHUNK = 128
B_HEAD = 64
LORA_W, LORA_A, LORA_G = 64, 64, 128
C_HEAD = 128
C_CHUNK = 128
LN_EPS = 1e-5
B_GN_EPS = 1e-5 * B_HEAD
RMS_EPS = 1e-6

RWKV_CHUNK = 64
RWKV_STEP_ROWS = 128
HGRN_SUB = 16
SAMPLE_ROWS = 16
ROW_BLOCK = 128
LANES = 128
SUBLANES = 8
MXU_TILE = 256
PROJ_TILE_N = 512
VMEM_LIMIT_BYTES = 56 * 1024 * 1024


def _cparams(*sem):
    return pltpu.CompilerParams(dimension_semantics=sem, vmem_limit_bytes=VMEM_LIMIT_BYTES)


def _pick_tile(n, target, align):
    best = None
    for t in range(align, min(n, target) + 1, align):
        if n % t == 0:
            best = t
    assert best is not None, (n, target, align)
    return best


def _dot(a, b):
    return jnp.dot(a, b, preferred_element_type=F32)


def _dot_nt(a, b):
    return lax.dot_general(a, b, (((1,), (1,)), ((), ())), preferred_element_type=F32)


def _dot_tn(a, b):
    return lax.dot_general(a, b, (((0,), (0,)), ((), ())), preferred_element_type=F32)


def _bdot(a, b):
    return jnp.dot(a.astype(BF16), b.astype(BF16), preferred_element_type=F32)


def _bdot_nt(a, b):
    return _dot_nt(a.astype(BF16), b.astype(BF16))


def _layer_norm(x, g, b, eps):
    mu = jnp.mean(x, axis=-1, keepdims=True)
    xc = x - mu
    var = jnp.mean(xc * xc, axis=-1, keepdims=True)
    return xc * lax.rsqrt(var + eps) * g + b


def _gelu(x):
    return 0.5 * x * (1.0 + lax.erf(x * 0.7071067811865476))


def _sigmoid(x):
    return 1.0 / (1.0 + jnp.exp(-x))


def _log_sigmoid(x):
    return jnp.minimum(x, 0.0) - jnp.log1p(jnp.exp(-jnp.abs(x)))


def _split_bf16(x):
    hi = x.astype(BF16).astype(F32)
    return hi, x - hi


def _seg_sum(x, ones_bd, two_term=True):
    rows, width = x.shape
    t = ones_bd.shape[0]
    nb = width // t
    hi = x.astype(BF16)
    terms = (hi, (x - hi.astype(F32)).astype(BF16)) if two_term else (hi,)
    parts = [p[:, j * t:(j + 1) * t] for p in terms for j in range(nb)]
    r = _dot(jnp.concatenate(parts, axis=0), ones_bd)
    cols = []
    for j in range(nb):
        c = r[j * rows:(j + 1) * rows]
        if two_term:
            c = c + r[(nb + j) * rows:(nb + j + 1) * rows]
        cols.append(c)
    return jnp.concatenate(cols, axis=1)


def _cumsum_rows(x, period=None):
    rows, width = x.shape
    hi = x.astype(BF16)
    r1 = x - hi.astype(F32)
    mid = r1.astype(BF16)
    lo = (r1 - mid.astype(F32)).astype(BF16)
    ri = lax.broadcasted_iota(jnp.int32, (rows, rows), 0)
    ci = lax.broadcasted_iota(jnp.int32, (rows, rows), 1)
    tri = ri >= ci
    if period is not None and period < rows:
        tri = tri & (ci >= (ri // period) * period)
    c = _dot(jnp.where(tri, 1.0, 0.0).astype(BF16), jnp.concatenate([hi, mid, lo], axis=1))
    return c[:, :width] + c[:, width:2 * width] + c[:, 2 * width:]


def _layer_block(arr, layer):
    nd = arr.ndim - 1
    return pl.BlockSpec((None,) + arr.shape[1:], lambda *_: (layer,) + (0,) * nd)


def _whole(arr):
    nd = arr.ndim
    return pl.BlockSpec(arr.shape, lambda *_: (0,) * nd)


def _ln_in_kernel(xp_ref, xs_ref, g_ref, b_ref, h_ref, hb_ref, *, n_prompt_blocks):
    i = pl.program_id(0)

    def emit(x):
        h = _layer_norm(x, g_ref[...], b_ref[...], LN_EPS)
        h_ref[...] = h
        hb_ref[...] = h.astype(BF16)

    @pl.when(i < n_prompt_blocks)
    def _():
        emit(xp_ref[...])

    @pl.when(i >= n_prompt_blocks)
    def _():
        emit(xs_ref[...])


def _ln_in(xp, xs, g, b):
    n_p, d = xp.shape
    n_s = xs.shape[0]
    npb, nsb = n_p // ROW_BLOCK, n_s // ROW_BLOCK
    m = n_p + n_s
    return pl.pallas_call(
        functools.partial(_ln_in_kernel, n_prompt_blocks=npb),
        grid=(npb + nsb,),
        in_specs=[
            pl.BlockSpec((ROW_BLOCK, d), lambda i: (jnp.minimum(i, npb - 1), 0)),
            pl.BlockSpec((ROW_BLOCK, d), lambda i: (jnp.maximum(i - npb, 0), 0)),
            pl.BlockSpec((1, d), lambda i: (0, 0)),
            pl.BlockSpec((1, d), lambda i: (0, 0)),
        ],
        out_specs=[pl.BlockSpec((ROW_BLOCK, d), lambda i: (i, 0)), pl.BlockSpec((ROW_BLOCK, d), lambda i: (i, 0))],
        out_shape=[jax.ShapeDtypeStruct((m, d), F32), jax.ShapeDtypeStruct((m, d), BF16)],
        compiler_params=_cparams("parallel"),
        name="ln_in",
    )(xp, xs, g, b)


def _mm_kernel(x_ref, w_ref, o_ref, wb_scr):
    @pl.when(pl.program_id(1) == 0)
    def _():
        wb_scr[...] = w_ref[...].astype(BF16)

    o_ref[...] = jnp.dot(x_ref[...], wb_scr[...], preferred_element_type=F32)


def _mm(xb, w, layer, col0, n, name):
    m, k = xb.shape
    tm = _pick_tile(m, 1664, 16)
    tn = PROJ_TILE_N
    assert col0 % tn == 0 and n % tn == 0
    j0 = col0 // tn
    return pl.pallas_call(
        _mm_kernel,
        grid=(n // tn, m // tm),
        in_specs=[pl.BlockSpec((tm, k), lambda j, i: (i, 0)),
                  pl.BlockSpec((None, k, tn), lambda j, i: (layer, 0, j0 + j))],
        out_specs=pl.BlockSpec((tm, tn), lambda j, i: (i, j)),
        out_shape=jax.ShapeDtypeStruct((m, n), F32),
        scratch_shapes=[pltpu.VMEM((k, tn), BF16)],
        compiler_params=_cparams("parallel", "arbitrary"),
        name=name,
    )(xb, w)


def _sgu_kernel(u_ref, v_ref, ln_ref, ws_ref, bst_ref, y_ref, vn_ref, *, n_prompt_blocks):
    i = pl.program_id(0)
    gd = u_ref.shape[1] // A_GROUPS
    u = _gelu(u_ref[...])
    v = _gelu(v_ref[...])
    vn = [
        _layer_norm(v[:, g * gd:(g + 1) * gd], ln_ref[0:1, g * gd:(g + 1) * gd], ln_ref[1:2, g * gd:(g + 1) * gd], LN_EPS)
        for g in range(A_GROUPS)
    ]

    @pl.when(i < n_prompt_blocks)
    def _():
        row = lax.broadcasted_iota(jnp.int32, (A_CHUNK, A_CHUNK), 0)
        col = lax.broadcasted_iota(jnp.int32, (A_CHUNK, A_CHUNK), 1)
        for g in range(A_GROUPS):
            w_causal = jnp.where(row >= col, ws_ref[g], 0.0)
            mixed = _bdot(w_causal, vn[g]) + bst_ref[:, g:g + 1]
            y_ref[:, g * gd:(g + 1) * gd] = (u[:, g * gd:(g + 1) * gd] * mixed).astype(BF16)

    @pl.when(i >= n_prompt_blocks)
    def _():
        for g in range(A_GROUPS):
            mixed = vn[g] * ws_ref[g, 0:1, 0:1] + bst_ref[0:1, g:g + 1]
            y_ref[:, g * gd:(g + 1) * gd] = (u[:, g * gd:(g + 1) * gd] * mixed).astype(BF16)
            vn_ref[:, g * gd:(g + 1) * gd] = vn[g]


def _sgu(za, ln, ws, bst, layer, n_prompt):
    m = za.shape[0]
    wa = za.shape[1] // 2
    npb = n_prompt // A_CHUNK
    nb = m // A_CHUNK
    n_s = m - n_prompt
    return pl.pallas_call(
        functools.partial(_sgu_kernel, n_prompt_blocks=npb),
        grid=(nb,),
        in_specs=[
            pl.BlockSpec((A_CHUNK, wa), lambda i: (i, 0)),
            pl.BlockSpec((A_CHUNK, wa), lambda i: (i, 1)),
            _layer_block(ln, layer), _layer_block(ws, layer), _layer_block(bst, layer),
        ],
        out_specs=[
            pl.BlockSpec((A_CHUNK, wa), lambda i: (i, 0)),
            pl.BlockSpec((A_CHUNK, wa), lambda i: (jnp.maximum(i - npb, 0), 0)),
        ],
        out_shape=[jax.ShapeDtypeStruct((m, wa), BF16), jax.ShapeDtypeStruct((n_s, wa), F32)],
        compiler_params=_cparams("arbitrary"),
        name="sgu",
    )(za, za, ln, ws, bst)


_V_W0, _V_A0, _V_KK, _V_KA, _V_RK, _V_GNG, _V_GNB = range(7)


def _rwkv_prep(zb, prev, mu, vec, wup, aup, gup, ones_bd):
    wb = vec.shape[1]
    row = lambda j: vec[j:j + 1, :]
    xs = zb + mu * (prev - zb)
    r, k, v = xs[:, :wb], xs[:, wb:2 * wb], xs[:, 2 * wb:3 * wb]
    o4, o5 = 3 * wb + LORA_W, 3 * wb + LORA_W + LORA_A
    wd, ad, gd = xs[:, 3 * wb:o4], xs[:, o4:o5], xs[:, o5:]
    w = row(_V_W0) + _bdot(jnp.tanh(wd), wup)
    softplus_neg_w = jnp.maximum(-w, 0.0) + jnp.log1p(jnp.exp(-jnp.abs(w)))
    log_decay = -jnp.exp(-softplus_neg_w - 0.5)
    a = _sigmoid(row(_V_A0) + _bdot(ad, aup))
    g = _bdot(_sigmoid(gd), gup)
    kk = k * row(_V_KK)
    kk = kk / jnp.maximum(jnp.sqrt(_seg_sum(kk * kk, ones_bd)), 1e-12)
    kd = k * (1.0 + (a - 1.0) * row(_V_KA))
    bonus = _seg_sum(r * kd * row(_V_RK), ones_bd) * v
    return r, log_decay, kd, v, kk, a, g, bonus


def _rwkv_finish(y, bonus, g, vec, ones_bd):
    mean = _seg_sum(y, ones_bd, two_term=False) * (1.0 / B_HEAD)
    yc = y - mean
    var = _seg_sum(yc * yc, ones_bd, two_term=False) * (1.0 / B_HEAD)
    return (yc * lax.rsqrt(var + B_GN_EPS) * vec[_V_GNG:_V_GNG + 1, :] + vec[_V_GNB:_V_GNB + 1, :] + bonus) * g


_FAC_NAMES = ("rp", "kkp", "kp", "bp", "rhat", "kkhat", "ktil", "btil", "v")


def _rwkv_prompt_kernel(zb0_ref, zbn_ref, ys_ref, mu_ref, vec_ref, wup_ref, aup_ref, gup_ref, ones_ref,
                        y_ref, sout_ref, shift_ref, state_scr, prev_scr, fac_scr, gb_scr, ec_scr,
                        *, steps_per_seq, n_prompt_steps):
    i = pl.program_id(0)

    @pl.when(i >= n_prompt_steps)
    def _():
        y_ref[...] = ys_ref[...]

    @pl.when(i < n_prompt_steps)
    def _():
        s_idx = lax.rem(i, steps_per_seq)

        @pl.when(s_idx == 0)
        def _():
            state_scr[...] = jnp.zeros_like(state_scr)

        R = zbn_ref.shape[0]
        C = RWKV_CHUNK
        n_pairs = state_scr.shape[0]
        N = B_HEAD
        P = 2 * N
        ones_bd = ones_ref[...]
        vec = vec_ref[...]
        row1 = lax.broadcasted_iota(jnp.int32, (R, 1), 0)

        def prep_pieces(zb, prev_row):
            wb = vec.shape[1]
            vrow = lambda j: vec[j:j + 1, :]
            prev = jnp.where(row1 == 0, prev_row, pltpu.roll(zb, 1, 0))
            xs = zb + mu_ref[...] * (prev - zb)
            r, k, v = xs[:, :wb], xs[:, wb:2 * wb], xs[:, 2 * wb:3 * wb]
            o4, o5 = 3 * wb + LORA_W, 3 * wb + LORA_W + LORA_A
            wd, ad, gd = xs[:, 3 * wb:o4], xs[:, o4:o5], xs[:, o5:]
            yield
            w = vrow(_V_W0) + _bdot(jnp.tanh(wd), wup_ref[...])
            softplus_neg_w = jnp.maximum(-w, 0.0) + jnp.log1p(jnp.exp(-jnp.abs(w)))
            lw = -jnp.exp(-softplus_neg_w - 0.5)
            yield
            a = _sigmoid(vrow(_V_A0) + _bdot(ad, aup_ref[...]))
            g = _bdot(_sigmoid(gd), gup_ref[...])
            yield
            kk = k * vrow(_V_KK)
            kk = kk / jnp.maximum(jnp.sqrt(_seg_sum(kk * kk, ones_bd)), 1e-12)
            yield
            kd = k * (1.0 + (a - 1.0) * vrow(_V_KA))
            bonus = _seg_sum(r * kd * vrow(_V_RK), ones_bd) * v
            yield
            cum = _cumsum_rows(lw, C)
            yield
            facs = []
            for c in range(R // C):
                facs.append(chunk_factors(c, r, lw, kd, v, kk, a, cum))
                yield
            for j, name in enumerate(_FAC_NAMES):
                fac_scr[j] = jnp.concatenate([f[name] for f in facs], axis=0)
            for c, f in enumerate(facs):
                ec_scr[c:c + 1, :] = f["e_cend"]
            gb_scr[0] = g
            gb_scr[1] = bonus
            prev_scr[...] = zb[R - 1:R, :]

        lane = lax.broadcasted_iota(jnp.int32, (C, P), 1)
        rowc = lax.broadcasted_iota(jnp.int32, (C, P), 0)
        lo = lane < N
        col_in = jnp.where(lo, lane, lane - N)
        incl2 = rowc >= col_in
        strict2 = rowc > col_in
        lo2 = lax.broadcasted_iota(jnp.int32, (2 * C, P), 1) < N
        bd_mask = (lax.broadcasted_iota(jnp.int32, (P, P), 0) < N) == (lax.broadcasted_iota(jnp.int32, (P, P), 1) < N)
        pairs = [slice(p * P, (p + 1) * P) for p in range(n_pairs)]

        def chunk_factors(c, r_all, lw_all, kd_all, v_all, kk_all, a_all, cum_all):
            rs = slice(c * C, (c + 1) * C)
            r, lw, kd, v, kk, a, cum = (t[rs] for t in (r_all, lw_all, kd_all, v_all, kk_all, a_all, cum_all))
            m = cum[C // 2 - 1:C // 2, :]
            cend = cum[C - 1:C, :]
            e_pos = jnp.exp(cum - m)
            e_neg = jnp.exp(m - cum)
            e_prev = jnp.exp(cum - lw - m)
            e_m = jnp.exp(m)
            e_end = jnp.exp(cend - m)
            rp = r * e_pos
            kkp = kk * e_prev
            kp = kd * e_neg
            bp = kk * a * e_neg
            rhat = rp * e_m
            kkhat = kkp * e_m
            ktil = kp * e_end
            btil = bp * e_end
            e_cend = e_end * e_m
            return dict(rp=rp, kkp=kkp, kp=kp, bp=bp, rhat=rhat, kkhat=kkhat, ktil=ktil, btil=btil, v=v,
                        e_cend=e_cend)

        def local_units(facs, between_stages):
            units = [(f, ps) for f in facs for ps in pairs]
            scores = []
            for f, ps in units:
                q_pair = jnp.concatenate([f["rp"][:, ps], f["kkp"][:, ps]], axis=0)
                q1 = jnp.where(lo2, q_pair, 0.0).astype(BF16)
                q2 = jnp.where(lo2, 0.0, q_pair).astype(BF16)
                kp_b, bp_b = f["kp"][:, ps].astype(BF16), f["bp"][:, ps].astype(BF16)
                scores.append((_dot_nt(q1, jnp.concatenate([bp_b, kp_b], axis=0)),
                               _dot_nt(q2, jnp.concatenate([kp_b, bp_b], axis=0))))
            between_stages()
            ar1, ar2, n_bd, avs, v_b = [], [], [], [], []
            for (f, ps), (sc1, sc2) in zip(units, scores):
                ar1.append(jnp.where(incl2, sc1[:C], 0.0).astype(BF16))
                ar2.append(jnp.where(incl2, sc2[:C], 0.0).astype(BF16))
                kn1 = jnp.where(strict2, sc1[C:], 0.0)
                kn2 = jnp.where(strict2, sc2[C:], 0.0)
                n_bd.append(jnp.concatenate([jnp.where(lo, kn1, 0.0), jnp.where(lo, 0.0, kn2)], axis=0).astype(BF16))
                akk = jnp.concatenate([jnp.where(lo, 0.0, kn1), jnp.where(lo, kn2, 0.0)], axis=0).astype(BF16)
                v_pair = f["v"][:, ps]
                v_b.append(v_pair.astype(BF16))
                v_sw = pltpu.roll(v_pair, N, 1).astype(BF16)
                avs.append(_dot(akk, jnp.concatenate([v_sw, v_sw], axis=0)))
            between_stages()
            xs = []
            for (f, ps), av in zip(units, avs):
                kkh = f["kkhat"][:, ps]
                xs.append(jnp.concatenate([jnp.where(lo, kkh, av[:C]), jnp.where(lo, av[C:], kkh)], axis=0))
            pw = n_bd
            span = 1
            while span < C:
                last = 2 * span >= C
                rhs = [x.astype(BF16) if last else jnp.concatenate([t, x.astype(BF16)], axis=1)
                       for t, x in zip(pw, xs)]
                prod = [_dot(t, r) for t, r in zip(pw, rhs)]
                px = prod if last else [r[:, P:] for r in prod]
                xs = [x - d for x, d in zip(xs, px)] if span == 1 else [x + d for x, d in zip(xs, px)]
                if not last:
                    pw = [r[:, :P].astype(BF16) for r in prod]
                between_stages()
                span *= 2
            zeros_cp = jnp.zeros((C, P), BF16)
            zs, w_parts = [], []
            for (f, ps), x, a1, a2, vb in zip(units, xs, ar1, ar2, v_b):
                w1 = jnp.where(lo, x[:C], x[C:])
                w2 = pltpu.roll(jnp.where(lo, x[C:], x[:C]), N, 1)
                w1_hi, w1_lo = _split_bf16(w1)
                w2_hi, w2_lo = _split_bf16(w2)
                nw = jnp.concatenate([-w1_hi, -w2_hi], axis=1).astype(BF16)
                zv = jnp.concatenate([zeros_cp, vb], axis=1)
                zs.append((_dot(a1, jnp.concatenate([nw, zv], axis=0)),
                           _dot(a2, jnp.concatenate([zv, nw], axis=0))))
                w_parts.append((w1_hi, w1_lo, w2_hi, w2_lo))
            out = []
            for (f, ps), (z1, z2), (w1_hi, w1_lo, w2_hi, w2_lo), vb in zip(units, zs, w_parts, v_b):
                qeff = (f["rhat"][:, ps] + jnp.where(lo, z1[:, :P], z2[:, :P])).astype(BF16)
                yloc = jnp.where(lo, z1[:, P:], z2[:, P:])
                kt_b, bt_b = f["ktil"][:, ps].astype(BF16), f["btil"][:, ps].astype(BF16)
                w1tb = _dot_tn(jnp.concatenate([w1_hi, w1_lo], axis=0).astype(BF16),
                               jnp.concatenate([bt_b, bt_b], axis=0))
                hm = _dot_tn(jnp.concatenate([vb, (-w2_hi).astype(BF16), (-w2_lo).astype(BF16)], axis=0),
                             jnp.concatenate([kt_b, bt_b, bt_b], axis=0))
                out.append(dict(qeff=qeff, yloc=yloc, g_bd=jnp.where(bd_mask, -w1tb, 0.0).astype(BF16),
                                h_bd=jnp.where(bd_mask, hm, 0.0), e_cend=f["e_cend"][:, ps]))
            return [out[c * n_pairs:(c + 1) * n_pairs] for c in range(len(facs))]

        def state_part(loc, states):
            ys, new_states = [], []
            for u, s_bd in zip(loc, states):
                s_hi, s_lo = _split_bf16(s_bd)
                s_hi_b = s_hi.astype(BF16)
                ys.append(_dot_nt(u["qeff"], s_hi_b) + u["yloc"])
                g2 = jnp.concatenate([u["g_bd"], u["g_bd"]], axis=0)
                sg = _dot(jnp.concatenate([s_hi_b, s_lo.astype(BF16)], axis=1), g2)
                new_states.append(s_bd * u["e_cend"] + sg + u["h_bd"])
            return jnp.concatenate(ys, axis=1), new_states

        @pl.when(i == 0)
        def _():
            for _ in prep_pieces(zb0_ref[...], jnp.zeros_like(prev_scr)):
                pass

        last_row = prev_scr[...]
        facs = []
        for c in range(R // C):
            f = {name: fac_scr[j, c * C:(c + 1) * C, :] for j, name in enumerate(_FAC_NAMES)}
            f["e_cend"] = ec_scr[c:c + 1, :]
            facs.append(f)
        g, bonus = gb_scr[0], gb_scr[1]

        next_starts_seq = lax.rem(i + 1, steps_per_seq) == 0
        next_prep = prep_pieces(zbn_ref[...], jnp.where(next_starts_seq, 0.0, last_row))
        locs = local_units(facs, lambda: next(next_prep, None))
        states = [state_scr[p] for p in range(n_pairs)]
        y_rows = []
        for loc in locs:
            y_c, states = state_part(loc, states)
            y_rows.append(y_c)
        for p in range(n_pairs):
            state_scr[p] = states[p]

        y = jnp.concatenate(y_rows, axis=0)
        y_ref[...] = _rwkv_finish(y, bonus, g, vec, ones_bd).astype(BF16)
        for _ in next_prep:
            pass

        @pl.when(s_idx == steps_per_seq - 1)
        def _():
            shift_ref[0] = last_row
            for p in range(n_pairs):
                sout_ref[0, 2 * p] = states[p][:N, :N]
                sout_ref[0, 2 * p + 1] = states[p][N:, N:]


def _rwkv_prompt(zb, y_sample, params, layer, batch, seq):
    mu, vec, wup, aup, gup, ones_bd = params
    pb = zb.shape[1]
    wb = vec.shape[2]
    n_heads = wb // B_HEAD
    R = RWKV_STEP_ROWS
    sps = seq // R
    n_steps = batch * sps
    n_s = y_sample.shape[0]
    m = batch * seq + n_s
    seq_of = lambda i: jnp.minimum(i // sps, batch - 1)
    return pl.pallas_call(
        functools.partial(_rwkv_prompt_kernel, steps_per_seq=sps, n_prompt_steps=n_steps),
        grid=(n_steps + n_s // R,),
        in_specs=[
            pl.BlockSpec((R, pb), lambda i: (0, 0), pipeline_mode=pl.Buffered(1)),
            pl.BlockSpec((R, pb), lambda i: (jnp.minimum(i + 1, n_steps - 1), 0)),
            pl.BlockSpec((R, wb), lambda i: (jnp.maximum(i - n_steps, 0), 0)),
            _layer_block(mu, layer), _layer_block(vec, layer), _layer_block(wup, layer),
            _layer_block(aup, layer), _layer_block(gup, layer), _whole(ones_bd),
        ],
        out_specs=[
            pl.BlockSpec((R, wb), lambda i: (i, 0)),
            pl.BlockSpec((1, n_heads, B_HEAD, B_HEAD), lambda i: (seq_of(i), 0, 0, 0)),
            pl.BlockSpec((1, 1, pb), lambda i: (seq_of(i), 0, 0)),
        ],
        out_shape=[
            jax.ShapeDtypeStruct((m, wb), BF16),
            jax.ShapeDtypeStruct((batch, n_heads, B_HEAD, B_HEAD), F32),
            jax.ShapeDtypeStruct((batch, 1, pb), F32),
        ],
        scratch_shapes=[pltpu.VMEM((n_heads // 2, 2 * B_HEAD, 2 * B_HEAD), F32), pltpu.VMEM((1, pb), F32),
                        pltpu.VMEM((len(_FAC_NAMES), R, wb), F32), pltpu.VMEM((2, R, wb), F32),
                        pltpu.VMEM((SUBLANES, wb), F32)],
        compiler_params=_cparams("arbitrary"),
        name="rwkv_prompt",
    )(zb, zb, y_sample, mu, vec, wup, aup, gup, ones_bd)


def _rwkv_sample_kernel(zb_ref, shift_ref, mu_ref, vec_ref, wup_ref, aup_ref, gup_ref, ones_ref, s_ref, buf_ref,
                        y_ref, sout_ref, shift_out_ref,
                        kk_t, wr_t, b_t, kd_t, v_t, w_t, r_t, y_t, g_scr, bonus_scr):
    del buf_ref
    p = pl.program_id(1)
    heads_per_step = s_ref.shape[0]
    N = B_HEAD

    @pl.when(p == 0)
    def _():
        zb = zb_ref[...]
        shift_out_ref[...] = zb
        r, lw, kd, v, kk, a, g, bonus = _rwkv_prep(
            zb, shift_ref[...], mu_ref[...], vec_ref[...], wup_ref[...], aup_ref[...], gup_ref[...], ones_ref[...])
        w = jnp.exp(lw)
        kk_t[...] = kk.T
        wr_t[...] = (w * r).T
        b_t[...] = (kk * a).T
        kd_t[...] = kd.T
        v_t[...] = v.T
        w_t[...] = w.T
        r_t[...] = r.T
        g_scr[...] = g
        bonus_scr[...] = bonus

    for hh in range(heads_per_step):
        base = pl.multiple_of((p * heads_per_step + hh) * N, N)
        hs = pl.ds(base, N)
        kk_h, wr_h, b_h, kd_h, w_h, r_h = kk_t[hs, :], wr_t[hs, :], b_t[hs, :], kd_t[hs, :], w_t[hs, :], r_t[hs, :]
        b_dot_r = jnp.sum(b_h * r_h, axis=0, keepdims=True)
        k_dot_r = jnp.sum(kd_h * r_h, axis=0, keepdims=True)

        def body(vi, carry):
            s0 = s_ref[hh, vi]
            s_kk = jnp.sum(s0 * kk_h, axis=0, keepdims=True)
            yq = jnp.sum(s0 * wr_h, axis=0, keepdims=True)
            v_row = v_t[pl.ds(base + vi, 1), :]
            sout_ref[hh, vi] = s0 * w_h - s_kk * b_h + v_row * kd_h
            y_t[pl.ds(base + vi, 1), :] = yq - s_kk * b_dot_r + v_row * k_dot_r
            return carry

        lax.fori_loop(0, N, body, 0, unroll=4)

    @pl.when(p == pl.num_programs(1) - 1)
    def _():
        y = y_t[...].T
        y_ref[...] = _rwkv_finish(y, bonus_scr[...], g_scr[...], vec_ref[...], ones_ref[...]).astype(BF16)


def _rwkv_sample(zb, shift, state_t, out_buf, params, layer, n_prompt):
    mu, vec, wup, aup, gup, ones_bd = params
    pb = zb.shape[1]
    wb = vec.shape[2]
    n_heads, n_s = state_t.shape[1], state_t.shape[4]
    hps = 2
    off = n_prompt // LANES
    sspec = pl.BlockSpec((None, hps, B_HEAD, B_HEAD, LANES), lambda sb, p: (layer, p, 0, 0, sb))
    return pl.pallas_call(
        _rwkv_sample_kernel,
        grid=(n_s // LANES, n_heads // hps),
        in_specs=[
            pl.BlockSpec((LANES, pb), lambda sb, p: (off + sb, 0)),
            pl.BlockSpec((None, LANES, pb), lambda sb, p: (layer, sb, 0)),
            _layer_block(mu, layer), _layer_block(vec, layer), _layer_block(wup, layer),
            _layer_block(aup, layer), _layer_block(gup, layer), _whole(ones_bd),
            sspec, pl.BlockSpec(memory_space=pl.ANY),
        ],
        out_specs=[pl.BlockSpec((LANES, wb), lambda sb, p: (sb, 0)), sspec,
                   pl.BlockSpec((LANES, pb), lambda sb, p: (sb, 0))],
        out_shape=[jax.ShapeDtypeStruct((n_s, wb), BF16), jax.ShapeDtypeStruct(state_t.shape, F32),
                   jax.ShapeDtypeStruct((n_s, pb), F32)],
        input_output_aliases={9: 1},
        scratch_shapes=[pltpu.VMEM((wb, LANES), F32)] * 8 + [pltpu.VMEM((LANES, wb), F32)] * 2,
        compiler_params=_cparams("arbitrary", "arbitrary"),
        name="rwkv_sample",
    )(zb, shift, mu, vec, wup, aup, gup, ones_bd, state_t, out_buf)


def _hgrn_gates(fz, clb, layer):
    tail = jnp.log1p(jnp.exp(-jnp.abs(fz)))
    ls_pos = jnp.minimum(fz, 0.0) - tail
    ls_neg = jnp.minimum(-fz, 0.0) - tail
    sig_neg = jnp.exp(ls_neg)
    if layer == 0:
        return ls_pos, sig_neg
    e = jnp.exp(clb - jnp.max(clb, axis=0, keepdims=True))
    sm = e / jnp.sum(e, axis=0, keepdims=True)
    lb = jnp.sum(sm[1:layer + 1], axis=0, keepdims=True)
    x2 = jnp.log(lb) + ls_neg
    log_f = jnp.maximum(ls_pos, x2) + jnp.log1p(jnp.exp(-jnp.abs(ls_pos - x2)))
    return log_f, (1.0 - lb) * sig_neg


def _hgrn_finish(o, g, norm_g):
    on = o * lax.rsqrt(jnp.mean(o * o, axis=-1, keepdims=True) + RMS_EPS) * norm_g
    return on * (g * _sigmoid(g))


def _hgrn_prompt_kernel(zc_ref, ys_ref, clb_ref, ng_ref, y_ref, sout_ref, state_scr, *, layer, n_chunks, n_prompt_steps):
    i = pl.program_id(0)

    @pl.when(i >= n_prompt_steps)
    def _():
        y_ref[...] = ys_ref[...]

    @pl.when(i < n_prompt_steps)
    def _():
        c_idx = lax.rem(i, n_chunks)

        @pl.when(c_idx == 0)
        def _():
            state_scr[...] = jnp.zeros_like(state_scr)

        z = zc_ref[...]
        C = z.shape[0]
        wc = z.shape[1] // 4
        n_heads = wc // C_HEAD
        D = C_HEAD
        SUB = HGRN_SUB
        HALF = SUBLANES
        nsub = C // SUB
        q, fz, iv, g = z[:, :wc], z[:, wc:2 * wc], z[:, 2 * wc:3 * wc], z[:, 3 * wc:]
        log_f, kg = _hgrn_gates(fz, clb_ref[...], layer)
        bcum = _cumsum_rows(log_f)
        e_b = jnp.exp(bcum)
        b_end = bcum[C - 1:C, :]
        e_end = jnp.exp(b_end)
        k_hat = kg * jnp.exp(b_end - bcum)
        t_full = lax.broadcasted_iota(jnp.int32, (nsub, SUB, 1), 1)
        t_half = lax.broadcasted_iota(jnp.int32, (nsub, SUB - HALF, 1), 1) + HALF
        level_sizes = [SUB << li for li in range((C // SUB).bit_length() - 1)]
        half = C // 2
        n_lv = len(level_sizes) * half
        ri = lax.broadcasted_iota(jnp.int32, (n_lv, n_lv), 0)
        ci = lax.broadcasted_iota(jnp.int32, (n_lv, n_lv), 1)
        level_mask = None
        for li, s in enumerate(level_sizes):
            shift = s.bit_length() - 1
            in_level = (ri >= li * half) & (ri < (li + 1) * half) & (ci >= li * half) & (ci < (li + 1) * half)
            same_block = ((ri - li * half) >> shift) == ((ci - li * half) >> shift)
            lm = in_level & same_block
            level_mask = lm if level_mask is None else (level_mask | lm)

        outs = []
        for h in range(n_heads):
            sl = slice(h * D, (h + 1) * D)
            qh, kh, vh, bh = q[:, sl], kg[:, sl], iv[:, sl], bcum[:, sl]
            st = state_scr[h]
            o = _bdot_nt(qh * e_b[:, sl], st)
            q3, k3, v3, b3 = (t.reshape(nsub, SUB, D) for t in (qh, kh, vh, bh))
            q3h, b3h = q3[:, HALF:, :], b3[:, HALF:, :]
            od = jnp.zeros((nsub, SUB, D), F32)
            odh = jnp.zeros((nsub, SUB - HALF, D), F32)
            for j in range(SUB):
                if j < HALF:
                    dec = jnp.exp(b3 - b3[:, j:j + 1, :])
                    att = jnp.sum(q3 * k3[:, j:j + 1, :] * dec, axis=-1, keepdims=True)
                    od = od + jnp.where(t_full >= j, att, 0.0) * v3[:, j:j + 1, :]
                else:
                    dec = jnp.exp(b3h - b3[:, j:j + 1, :])
                    att = jnp.sum(q3h * k3[:, j:j + 1, :] * dec, axis=-1, keepdims=True)
                    odh = odh + jnp.where(t_half >= j, att, 0.0) * v3[:, j:j + 1, :]
            od = od + jnp.concatenate([jnp.zeros((nsub, HALF, D), F32), odh], axis=1)
            o = o + od.reshape(C, D)
            q_l, k_l, v_l = [], [], []
            for s in level_sizes:
                for j in range(C // (2 * s)):
                    a0 = 2 * j * s
                    b_bound = bh[a0 + s - 1:a0 + s, :]
                    q_l.append(qh[a0 + s:a0 + 2 * s, :] * jnp.exp(bh[a0 + s:a0 + 2 * s, :] - b_bound))
                    k_l.append(kh[a0:a0 + s, :] * jnp.exp(b_bound - bh[a0:a0 + s, :]))
                    v_l.append(vh[a0:a0 + s, :])
            att = _bdot_nt(jnp.concatenate(q_l, axis=0), jnp.concatenate(k_l, axis=0))
            o_lv = _bdot(jnp.where(level_mask, att, 0.0), jnp.concatenate(v_l, axis=0))
            for li, s in enumerate(level_sizes):
                pieces = []
                for j in range(C // (2 * s)):
                    r0 = li * (C // 2) + j * s
                    pieces += [jnp.zeros((s, D), F32), o_lv[r0:r0 + s, :]]
                o = o + jnp.concatenate(pieces, axis=0)
            outs.append(o)
            vh_hi, vh_lo = _split_bf16(vh)
            kh_hi, kh_lo = _split_bf16(k_hat[:, sl])
            upd = _dot_tn(jnp.concatenate([vh_hi, vh_hi, vh_lo], axis=0).astype(BF16),
                          jnp.concatenate([kh_hi, kh_lo, kh_hi], axis=0).astype(BF16))
            state_scr[h] = st * e_end[:, sl] + upd

        o_all = jnp.concatenate(outs, axis=1)
        y_ref[...] = _hgrn_finish(o_all, g, ng_ref[...]).astype(BF16)

        @pl.when(c_idx == n_chunks - 1)
        def _():
            for h in range(n_heads):
                sout_ref[0, h] = state_scr[h].T


def _hgrn_prompt(zc, y_sample, clb, norm_g, layer, batch, seq):
    pc = zc.shape[1]
    wc = pc // 4
    n_heads = wc // C_HEAD
    C = C_CHUNK
    nc = seq // C
    n_steps = batch * nc
    n_s = y_sample.shape[0]
    m = batch * seq + n_s
    return pl.pallas_call(
        functools.partial(_hgrn_prompt_kernel, layer=layer, n_chunks=nc, n_prompt_steps=n_steps),
        grid=(n_steps + n_s // C,),
        in_specs=[
            pl.BlockSpec((C, pc), lambda i: (jnp.minimum(i, n_steps - 1), 0)),
            pl.BlockSpec((C, wc), lambda i: (jnp.maximum(i - n_steps, 0), 0)),
            _whole(clb), _layer_block(norm_g, layer),
        ],
        out_specs=[
            pl.BlockSpec((C, wc), lambda i: (i, 0)),
            pl.BlockSpec((1, n_heads, C_HEAD, C_HEAD), lambda i: (jnp.minimum(i // nc, batch - 1), 0, 0, 0)),
        ],
        out_shape=[
            jax.ShapeDtypeStruct((m, wc), BF16),
            jax.ShapeDtypeStruct((batch, n_heads, C_HEAD, C_HEAD), F32),
        ],
        scratch_shapes=[pltpu.VMEM((n_heads, C_HEAD, C_HEAD), F32)],
        compiler_params=_cparams("arbitrary"),
        name="hgrn_prompt",
    )(zc, y_sample, clb, norm_g)


def _hgrn_sample_kernel(zc_ref, clb_ref, ng_ref, s_ref, buf_ref, y_ref, sout_ref,
                        q_scr, f_scr, k_scr, v_scr, o_scr, *, layer):
    del buf_ref
    z = zc_ref[...]
    n_rows, n_heads = s_ref.shape[0], s_ref.shape[1]
    wc = z.shape[1] // 4
    D = C_HEAD
    q, fz, iv, g = z[:, :wc], z[:, wc:2 * wc], z[:, 2 * wc:3 * wc], z[:, 3 * wc:]
    log_f, kg = _hgrn_gates(fz, clb_ref[...], layer)
    q_scr[...] = q
    f_scr[...] = jnp.exp(log_f)
    k_scr[...] = kg
    v_scr[...] = iv
    pad = jnp.zeros((SUBLANES - n_heads, D), F32)

    def body(i, carry):
        q_r, f_r, k_r, v_r = (t[pl.ds(i, 1), :] for t in (q_scr, f_scr, k_scr, v_scr))
        heads = lambda t: jnp.concatenate([t[:, h * D:(h + 1) * D] for h in range(n_heads)] + [pad], axis=0)
        f_cols = heads(f_r).T
        k_cols = heads(k_r).T
        os_ = []
        for h in range(n_heads):
            sl = slice(h * D, (h + 1) * D)
            s0 = s_ref[i, h]
            qk = jnp.sum(q_r[:, sl] * k_r[:, sl], axis=-1, keepdims=True)
            os_.append(_bdot(q_r[:, sl] * f_r[:, sl], s0) + qk * v_r[:, sl])
            sout_ref[i, h] = s0 * f_cols[:, h:h + 1] + k_cols[:, h:h + 1] * v_r[:, sl]
        o_scr[pl.ds(i, 1), :] = jnp.concatenate(os_, axis=1)
        return carry

    lax.fori_loop(0, n_rows, body, 0, unroll=2)
    y_ref[...] = _hgrn_finish(o_scr[...], g, ng_ref[...]).astype(BF16)


def _hgrn_sample(zc, clb, norm_g, state, out_buf, layer, n_prompt):
    pc = zc.shape[1]
    wc = pc // 4
    n_s, n_heads = state.shape[1], state.shape[2]
    R = SAMPLE_ROWS
    off = n_prompt // R
    sspec = pl.BlockSpec((None, R, n_heads, C_HEAD, C_HEAD), lambda i: (layer, i, 0, 0, 0))
    return pl.pallas_call(
        functools.partial(_hgrn_sample_kernel, layer=layer),
        grid=(n_s // R,),
        in_specs=[pl.BlockSpec((R, pc), lambda i: (off + i, 0)), _whole(clb), _layer_block(norm_g, layer),
                  sspec, pl.BlockSpec(memory_space=pl.ANY)],
        out_specs=[pl.BlockSpec((R, wc), lambda i: (i, 0)), sspec],
        out_shape=[jax.ShapeDtypeStruct((n_s, wc), BF16), jax.ShapeDtypeStruct(state.shape, F32)],
        input_output_aliases={4: 1},
        scratch_shapes=[pltpu.VMEM((R, wc), F32)] * 5,
        compiler_params=_cparams("parallel"),
        name="hgrn_sample",
    )(zc, clb, norm_g, state, out_buf)


def _out_kernel(ya_ref, yb_ref, yc_ref, h_ref, wo_ref, ln_ref, p_ref, wpg_ref, wpp_ref, hb_ref, res_ref, *, alpha):
    wa, wb = ya_ref.shape[1], yb_ref.shape[1]
    mix = (_dot(ya_ref[...], wo_ref[:wa, :]) + _dot(yb_ref[...], wo_ref[wa:wa + wb, :])
           + _dot(yc_ref[...], wo_ref[wa + wb:, :]))
    h1 = _layer_norm(alpha * h_ref[...] + mix, ln_ref[0:1, :], ln_ref[1:2, :], LN_EPS)
    h1b = h1.astype(BF16)
    ple = _sigmoid(_dot(h1b, wpg_ref[...])) * _dot(p_ref[...], wpp_ref[...])
    hb_ref[...] = h1b
    res_ref[...] = alpha * h1 + ple


def _out_proj(ya, yb, yc, h, wo, ln, p, wpg, wpp, layer, alpha):
    m, d = h.shape
    tm = _pick_tile(m, 416, 16)
    rows = lambda a: pl.BlockSpec((tm, a.shape[-1]), lambda i: (i, 0))
    const = lambda a: pl.BlockSpec((None,) + a.shape[1:], lambda i: (layer, 0, 0), pipeline_mode=pl.Buffered(1))
    return pl.pallas_call(
        functools.partial(_out_kernel, alpha=alpha),
        grid=(m // tm,),
        in_specs=[rows(ya), rows(yb), rows(yc), rows(h), const(wo), const(ln),
                  pl.BlockSpec((None, tm, p.shape[-1]), lambda i: (layer, i, 0)), const(wpg), const(wpp)],
        out_specs=[rows(h), rows(h)],
        out_shape=[jax.ShapeDtypeStruct((m, d), BF16), jax.ShapeDtypeStruct((m, d), F32)],
        compiler_params=_cparams("parallel"),
        name="out_proj",
    )(ya, yb, yc, h, wo, ln, p, wpg, wpp)


def _ffn_kernel(hb_ref, res_ref, wg_ref, wu_ref, wd_ref, ln_ref, h_ref, hbo_ref, acc_ref):
    f = pl.program_id(1)

    @pl.when(f == 0)
    def _():
        acc_ref[...] = jnp.zeros_like(acc_ref)

    x = hb_ref[...]
    gate = _dot(x, wg_ref[...])
    up = _dot(x, wu_ref[...])
    act = (gate * _sigmoid(gate) * up).astype(BF16)
    acc_ref[...] += _dot(act, wd_ref[...])

    @pl.when(f == pl.num_programs(1) - 1)
    def _():
        h2 = _layer_norm(res_ref[...] + acc_ref[...], ln_ref[0:1, :], ln_ref[1:2, :], LN_EPS)
        h_ref[...] = h2
        hbo_ref[...] = h2.astype(BF16)


def _ffn(hb, res, wg, wu, wd, ln, layer):
    m, d = hb.shape
    dff = wg.shape[2]
    tm = _pick_tile(m, 640, 16)
    tf = _pick_tile(dff, 512, 128)
    return pl.pallas_call(
        _ffn_kernel,
        grid=(m // tm, dff // tf),
        in_specs=[
            pl.BlockSpec((tm, d), lambda i, f: (i, 0)),
            pl.BlockSpec((tm, d), lambda i, f: (i, 0)),
            pl.BlockSpec((None, d, tf), lambda i, f: (layer, 0, f)),
            pl.BlockSpec((None, d, tf), lambda i, f: (layer, 0, f)),
            pl.BlockSpec((None, tf, d), lambda i, f: (layer, f, 0)),
            pl.BlockSpec((None, 2, d), lambda i, f: (layer, 0, 0)),
        ],
        out_specs=[pl.BlockSpec((tm, d), lambda i, f: (i, 0)), pl.BlockSpec((tm, d), lambda i, f: (i, 0))],
        out_shape=[jax.ShapeDtypeStruct((m, d), F32), jax.ShapeDtypeStruct((m, d), BF16)],
        scratch_shapes=[pltpu.VMEM((tm, d), F32)],
        compiler_params=_cparams("parallel", "arbitrary"),
        name="ffn",
    )(hb, res, wg, wu, wd, ln)


def kernel(x_prompt, x_sample, state_rwkv, state_shift, state_hgrn, p_prompt, p_sample, ln_in_g, ln_in_b, w_in, a_ln_g, a_ln_b, a_ws, a_bs, b_mu, b_w0, b_w_up, b_a0, b_a_up, b_g_up, b_k_k, b_k_a, b_r_k, b_gn_g, b_gn_b, c_lower_bounds, c_norm_g, w_out, ln1_g, ln1_b, w_ffn_gate, w_ffn_up, w_ffn_down, w_ple_gate, w_ple_proj, ln2_g, ln2_b):
    batch, seq, d = x_prompt.shape
    n_s = x_sample.shape[0]
    depth = w_in.shape[0]
    n_p = batch * seq
    wa, wb, wc = a_ln_g.shape[1], b_w0.shape[1], c_norm_g.shape[1]
    pa, pb = 2 * wa, 3 * wb + LORA_W + LORA_A + LORA_G
    alpha = float((2 * depth) ** 0.25)
    assert x_sample.shape[1] == 1 and seq % A_CHUNK == 0 and seq % C_CHUNK == 0 and seq % RWKV_STEP_ROWS == 0
    assert n_s % ROW_BLOCK == 0 and n_p % ROW_BLOCK == 0 and n_s % LANES == 0
    assert (wb // B_HEAD) % 2 == 0 and wb % MXU_TILE == 0

    bf = lambda t: t.astype(BF16)
    pc = w_in.shape[2] - pa - pb
    w_out_b, w_gate_b, w_up_b, w_down_b = bf(w_out), bf(w_ffn_gate), bf(w_ffn_up), bf(w_ffn_down)
    w_pg_b, w_pp_b = bf(w_ple_gate), bf(w_ple_proj)
    p_all = bf(jnp.concatenate([p_prompt.reshape(depth, n_p, -1), p_sample.reshape(depth, n_s, -1)], axis=1))
    sgu_ln = jnp.stack([a_ln_g, a_ln_b], axis=1)
    sgu_bst = jnp.swapaxes(a_bs, 1, 2)
    rwkv_vec = jnp.stack([b_w0, b_a0, b_k_k, b_k_a, b_r_k.reshape(depth, wb), b_gn_g, b_gn_b, jnp.zeros_like(b_w0)], axis=1)
    idx = jnp.arange(MXU_TILE)
    ones_bd = bf(idx[:, None] // B_HEAD == idx[None, :] // B_HEAD)
    rwkv_params = (b_mu.reshape(depth, 1, pb), rwkv_vec, bf(b_w_up), bf(b_a_up), bf(b_g_up), ones_bd)
    hgrn_ng = c_norm_g.reshape(depth, 1, wc)
    ln1 = jnp.stack([ln1_g, ln1_b], axis=1)
    ln2 = jnp.stack([ln2_g, ln2_b], axis=1)
    row = lambda t: t.reshape(1, -1)

    h, hb = _ln_in(x_prompt.reshape(n_p, d), x_sample.reshape(n_s, d), row(ln_in_g), row(ln_in_b))

    state_rwkv_t = jnp.transpose(state_rwkv, (0, 2, 3, 4, 1))
    rwkv_s = jnp.zeros(state_rwkv_t.shape, F32)
    hgrn_s = jnp.zeros(state_hgrn.shape, F32)
    rwkv_p, shift_p, hgrn_p, shift_s, sgu_v = [], [], [], [], []
    for l in range(depth):
        za = _mm(hb, w_in, l, 0, pa, "proj_a")
        zb = _mm(hb, w_in, l, pa, pb, "proj_b")
        zc = _mm(hb, w_in, l, pa + pb, pc, "proj_c")

        ya, v_rows = _sgu(za, sgu_ln, a_ws, sgu_bst, l, n_p)
        yb_s, rwkv_s, sh_s = _rwkv_sample(zb, state_shift, state_rwkv_t, rwkv_s, rwkv_params, l, n_p)
        yb, r_p, sh_p = _rwkv_prompt(zb, yb_s, rwkv_params, l, batch, seq)
        yc_s, hgrn_s = _hgrn_sample(zc, c_lower_bounds, hgrn_ng, state_hgrn, hgrn_s, l, n_p)
        yc, c_p = _hgrn_prompt(zc, yc_s, c_lower_bounds, hgrn_ng, l, batch, seq)

        hb, res = _out_proj(ya, yb, yc, h, w_out_b, ln1, p_all, w_pg_b, w_pp_b, l, alpha)
        h, hb = _ffn(hb, res, w_gate_b, w_up_b, w_down_b, ln2, l)

        rwkv_p.append(r_p)
        shift_p.append(sh_p.reshape(batch, pb))
        hgrn_p.append(c_p)
        shift_s.append(sh_s)
        sgu_v.append(v_rows.reshape(n_s, 1, wa))

    return (h[:n_p].reshape(batch, seq, d), h[n_p:].reshape(n_s, 1, d), jnp.stack(rwkv_p), jnp.stack(shift_p),
            jnp.stack(hgrn_p), jnp.transpose(rwkv_s, (0, 4, 1, 2, 3)), jnp.stack(shift_s), hgrn_s, jnp.stack(sgu_v))
```

```python
import functools

import jax
import jax.numpy as jnp
from jax import lax
from jax.experimental import pallas as pl
from jax.experimental.pallas import tpu as pltpu

F32 = jnp.float32
BF16 = jnp.bfloat16

A_GROUPS = 4
A_CHUNK = 128
B_HEAD = 64
LORA_W, LORA_A, LORA_G = 64, 64, 128
C_HEAD = 128
C_CHUNK = 128
LN_EPS = 1e-5
B_GN_EPS = 1e-5 * B_HEAD
RMS_EPS = 1e-6

RWKV_CHUNK = 64
RWKV_STEP_ROWS = 128
HGRN_SUB = 8
SAMPLE_ROWS = 16
ROW_BLOCK = 128
LANES = 128
SUBLANES = 8
MXU_TILE = 256
PROJ_TILE_N = 512
VMEM_LIMIT_BYTES = 56 * 1024 * 1024


def _cparams(*sem):
    return pltpu.CompilerParams(dimension_semantics=sem, vmem_limit_bytes=VMEM_LIMIT_BYTES)


def _pick_tile(n, target, align):
    best = None
    for t in range(align, min(n, target) + 1, align):
        if n % t == 0:
            best = t
    assert best is not None, (n, target, align)
    return best


def _dot(a, b):
    return jnp.dot(a, b, preferred_element_type=F32)


def _dot_nt(a, b):
    return lax.dot_general(a, b, (((1,), (1,)), ((), ())), preferred_element_type=F32)


def _dot_tn(a, b):
    return lax.dot_general(a, b, (((0,), (0,)), ((), ())), preferred_element_type=F32)


def _bdot(a, b):
    return jnp.dot(a.astype(BF16), b.astype(BF16), preferred_element_type=F32)


def _bdot_nt(a, b):
    return _dot_nt(a.astype(BF16), b.astype(BF16))


def _layer_norm(x, g, b, eps):
    mu = jnp.mean(x, axis=-1, keepdims=True)
    xc = x - mu
    var = jnp.mean(xc * xc, axis=-1, keepdims=True)
    return xc * lax.rsqrt(var + eps) * g + b


def _gelu(x):
    return 0.5 * x * (1.0 + lax.erf(x * 0.7071067811865476))


def _sigmoid(x):
    return 1.0 / (1.0 + jnp.exp(-x))


def _log_sigmoid(x):
    return jnp.minimum(x, 0.0) - jnp.log1p(jnp.exp(-jnp.abs(x)))


def _split_bf16(x):
    hi = x.astype(BF16).astype(F32)
    return hi, x - hi


def _seg_sum(x, ones_bd, two_term=True):
    rows, width = x.shape
    t = ones_bd.shape[0]
    nb = width // t
    hi = x.astype(BF16)
    terms = (hi, (x - hi.astype(F32)).astype(BF16)) if two_term else (hi,)
    parts = [p[:, j * t:(j + 1) * t] for p in terms for j in range(nb)]
    r = _dot(jnp.concatenate(parts, axis=0), ones_bd)
    cols = []
    for j in range(nb):
        c = r[j * rows:(j + 1) * rows]
        if two_term:
            c = c + r[(nb + j) * rows:(nb + j + 1) * rows]
        cols.append(c)
    return jnp.concatenate(cols, axis=1)


def _cumsum_rows(x, period=None):
    rows, width = x.shape
    hi = x.astype(BF16)
    r1 = x - hi.astype(F32)
    mid = r1.astype(BF16)
    lo = (r1 - mid.astype(F32)).astype(BF16)
    ri = lax.broadcasted_iota(jnp.int32, (rows, rows), 0)
    ci = lax.broadcasted_iota(jnp.int32, (rows, rows), 1)
    tri = ri >= ci
    if period is not None and period < rows:
        tri = tri & (ci >= (ri // period) * period)
    c = _dot(jnp.where(tri, 1.0, 0.0).astype(BF16), jnp.concatenate([hi, mid, lo], axis=1))
    return c[:, :width] + c[:, width:2 * width] + c[:, 2 * width:]


def _layer_block(arr, layer):
    nd = arr.ndim - 1
    return pl.BlockSpec((None,) + arr.shape[1:], lambda *_: (layer,) + (0,) * nd)


def _whole(arr):
    nd = arr.ndim
    return pl.BlockSpec(arr.shape, lambda *_: (0,) * nd)


def _ln_in_kernel(xp_ref, xs_ref, g_ref, b_ref, h_ref, hb_ref, *, n_prompt_blocks):
    i = pl.program_id(0)

    def emit(x):
        h = _layer_norm(x, g_ref[...], b_ref[...], LN_EPS)
        h_ref[...] = h
        hb_ref[...] = h.astype(BF16)

    @pl.when(i < n_prompt_blocks)
    def _():
        emit(xp_ref[...])

    @pl.when(i >= n_prompt_blocks)
    def _():
        emit(xs_ref[...])


def _ln_in(xp, xs, g, b):
    n_p, d = xp.shape
    n_s = xs.shape[0]
    npb, nsb = n_p // ROW_BLOCK, n_s // ROW_BLOCK
    m = n_p + n_s
    return pl.pallas_call(
        functools.partial(_ln_in_kernel, n_prompt_blocks=npb),
        grid=(npb + nsb,),
        in_specs=[
            pl.BlockSpec((ROW_BLOCK, d), lambda i: (jnp.minimum(i, npb - 1), 0)),
            pl.BlockSpec((ROW_BLOCK, d), lambda i: (jnp.maximum(i - npb, 0), 0)),
            pl.BlockSpec((1, d), lambda i: (0, 0)),
            pl.BlockSpec((1, d), lambda i: (0, 0)),
        ],
        out_specs=[pl.BlockSpec((ROW_BLOCK, d), lambda i: (i, 0)), pl.BlockSpec((ROW_BLOCK, d), lambda i: (i, 0))],
        out_shape=[jax.ShapeDtypeStruct((m, d), F32), jax.ShapeDtypeStruct((m, d), BF16)],
        compiler_params=_cparams("parallel"),
        name="ln_in",
    )(xp, xs, g, b)


def _mm_kernel(x_ref, w_ref, o_ref, wb_scr):
    @pl.when(pl.program_id(1) == 0)
    def _():
        wb_scr[...] = w_ref[...].astype(BF16)

    o_ref[...] = jnp.dot(x_ref[...], wb_scr[...], preferred_element_type=F32)


def _mm(xb, w, layer, col0, n, name):
    m, k = xb.shape
    tm = _pick_tile(m, 1664, 16)
    tn = PROJ_TILE_N
    assert col0 % tn == 0 and n % tn == 0
    j0 = col0 // tn
    return pl.pallas_call(
        _mm_kernel,
        grid=(n // tn, m // tm),
        in_specs=[pl.BlockSpec((tm, k), lambda j, i: (i, 0)),
                  pl.BlockSpec((None, k, tn), lambda j, i: (layer, 0, j0 + j))],
        out_specs=pl.BlockSpec((tm, tn), lambda j, i: (i, j)),
        out_shape=jax.ShapeDtypeStruct((m, n), F32),
        scratch_shapes=[pltpu.VMEM((k, tn), BF16)],
        compiler_params=_cparams("parallel", "arbitrary"),
        name=name,
    )(xb, w)


def _sgu_kernel(u_ref, v_ref, ln_ref, ws_ref, bst_ref, y_ref, vn_ref, *, n_prompt_blocks):
    i = pl.program_id(0)
    gd = u_ref.shape[1] // A_GROUPS
    u = _gelu(u_ref[...])
    v = _gelu(v_ref[...])
    vn = [
        _layer_norm(v[:, g * gd:(g + 1) * gd], ln_ref[0:1, g * gd:(g + 1) * gd], ln_ref[1:2, g * gd:(g + 1) * gd], LN_EPS)
        for g in range(A_GROUPS)
    ]

    @pl.when(i < n_prompt_blocks)
    def _():
        row = lax.broadcasted_iota(jnp.int32, (A_CHUNK, A_CHUNK), 0)
        col = lax.broadcasted_iota(jnp.int32, (A_CHUNK, A_CHUNK), 1)
        for g in range(A_GROUPS):
            w_causal = jnp.where(row >= col, ws_ref[g], 0.0)
            mixed = _bdot(w_causal, vn[g]) + bst_ref[:, g:g + 1]
            y_ref[:, g * gd:(g + 1) * gd] = (u[:, g * gd:(g + 1) * gd] * mixed).astype(BF16)

    @pl.when(i >= n_prompt_blocks)
    def _():
        for g in range(A_GROUPS):
            mixed = vn[g] * ws_ref[g, 0:1, 0:1] + bst_ref[0:1, g:g + 1]
            y_ref[:, g * gd:(g + 1) * gd] = (u[:, g * gd:(g + 1) * gd] * mixed).astype(BF16)
            vn_ref[:, g * gd:(g + 1) * gd] = vn[g]


def _sgu(za, ln, ws, bst, layer, n_prompt):
    m = za.shape[0]
    wa = za.shape[1] // 2
    npb = n_prompt // A_CHUNK
    nb = m // A_CHUNK
    n_s = m - n_prompt
    return pl.pallas_call(
        functools.partial(_sgu_kernel, n_prompt_blocks=npb),
        grid=(nb,),
        in_specs=[
            pl.BlockSpec((A_CHUNK, wa), lambda i: (i, 0)),
            pl.BlockSpec((A_CHUNK, wa), lambda i: (i, 1)),
            _layer_block(ln, layer), _layer_block(ws, layer), _layer_block(bst, layer),
        ],
        out_specs=[
            pl.BlockSpec((A_CHUNK, wa), lambda i: (i, 0)),
            pl.BlockSpec((A_CHUNK, wa), lambda i: (jnp.maximum(i - npb, 0), 0)),
        ],
        out_shape=[jax.ShapeDtypeStruct((m, wa), BF16), jax.ShapeDtypeStruct((n_s, wa), F32)],
        compiler_params=_cparams("arbitrary"),
        name="sgu",
    )(za, za, ln, ws, bst)


_V_W0, _V_A0, _V_KK, _V_KA, _V_RK, _V_GNG, _V_GNB = range(7)


def _rwkv_prep(zb, prev, mu, vec, wup, aup, gup, ones_bd):
    wb = vec.shape[1]
    row = lambda j: vec[j:j + 1, :]
    xs = zb + mu * (prev - zb)
    r, k, v = xs[:, :wb], xs[:, wb:2 * wb], xs[:, 2 * wb:3 * wb]
    o4, o5 = 3 * wb + LORA_W, 3 * wb + LORA_W + LORA_A
    wd, ad, gd = xs[:, 3 * wb:o4], xs[:, o4:o5], xs[:, o5:]
    w = row(_V_W0) + _bdot(jnp.tanh(wd), wup)
    softplus_neg_w = jnp.maximum(-w, 0.0) + jnp.log1p(jnp.exp(-jnp.abs(w)))
    log_decay = -jnp.exp(-softplus_neg_w - 0.5)
    a = _sigmoid(row(_V_A0) + _bdot(ad, aup))
    g = _bdot(_sigmoid(gd), gup)
    kk = k * row(_V_KK)
    kk = kk / jnp.maximum(jnp.sqrt(_seg_sum(kk * kk, ones_bd)), 1e-12)
    kd = k * (1.0 + (a - 1.0) * row(_V_KA))
    bonus = _seg_sum(r * kd * row(_V_RK), ones_bd) * v
    return r, log_decay, kd, v, kk, a, g, bonus


def _rwkv_finish(y, bonus, g, vec, ones_bd):
    mean = _seg_sum(y, ones_bd, two_term=False) * (1.0 / B_HEAD)
    yc = y - mean
    var = _seg_sum(yc * yc, ones_bd, two_term=False) * (1.0 / B_HEAD)
    return (yc * lax.rsqrt(var + B_GN_EPS) * vec[_V_GNG:_V_GNG + 1, :] + vec[_V_GNB:_V_GNB + 1, :] + bonus) * g


_FAC_NAMES = ("rp", "kkp", "kp", "bp", "rhat", "kkhat", "ktil", "btil", "v")


def _rwkv_prompt_kernel(zb0_ref, zbn_ref, ys_ref, mu_ref, vec_ref, wup_ref, aup_ref, gup_ref, ones_ref,
                        y_ref, sout_ref, shift_ref, state_scr, prev_scr, fac_scr, gb_scr, ec_scr,
                        *, steps_per_seq, n_prompt_steps):
    i = pl.program_id(0)

    @pl.when(i >= n_prompt_steps)
    def _():
        y_ref[...] = ys_ref[...]

    @pl.when(i < n_prompt_steps)
    def _():
        s_idx = lax.rem(i, steps_per_seq)

        @pl.when(s_idx == 0)
        def _():
            state_scr[...] = jnp.zeros_like(state_scr)

        R = zbn_ref.shape[0]
        C = RWKV_CHUNK
        n_pairs = state_scr.shape[0]
        N = B_HEAD
        P = 2 * N
        ones_bd = ones_ref[...]
        vec = vec_ref[...]
        row1 = lax.broadcasted_iota(jnp.int32, (R, 1), 0)

        def prep_pieces(zb, prev_row):
            wb = vec.shape[1]
            vrow = lambda j: vec[j:j + 1, :]
            prev = jnp.where(row1 == 0, prev_row, pltpu.roll(zb, 1, 0))
            xs = zb + mu_ref[...] * (prev - zb)
            r, k, v = xs[:, :wb], xs[:, wb:2 * wb], xs[:, 2 * wb:3 * wb]
            o4, o5 = 3 * wb + LORA_W, 3 * wb + LORA_W + LORA_A
            wd, ad, gd = xs[:, 3 * wb:o4], xs[:, o4:o5], xs[:, o5:]
            yield
            w = vrow(_V_W0) + _bdot(jnp.tanh(wd), wup_ref[...])
            softplus_neg_w = jnp.maximum(-w, 0.0) + jnp.log1p(jnp.exp(-jnp.abs(w)))
            lw = -jnp.exp(-softplus_neg_w - 0.5)
            yield
            a = _sigmoid(vrow(_V_A0) + _bdot(ad, aup_ref[...]))
            g = _bdot(_sigmoid(gd), gup_ref[...])
            yield
            kk = k * vrow(_V_KK)
            kk = kk / jnp.maximum(jnp.sqrt(_seg_sum(kk * kk, ones_bd)), 1e-12)
            yield
            kd = k * (1.0 + (a - 1.0) * vrow(_V_KA))
            bonus = _seg_sum(r * kd * vrow(_V_RK), ones_bd) * v
            yield
            cum = _cumsum_rows(lw, C)
            yield
            facs = []
            for c in range(R // C):
                facs.append(chunk_factors(c, r, lw, kd, v, kk, a, cum))
                yield
            for j, name in enumerate(_FAC_NAMES):
                fac_scr[j] = jnp.concatenate([f[name] for f in facs], axis=0)
            for c, f in enumerate(facs):
                ec_scr[c:c + 1, :] = f["e_cend"]
            gb_scr[0] = g
            gb_scr[1] = bonus
            prev_scr[...] = zb[R - 1:R, :]

        lane = lax.broadcasted_iota(jnp.int32, (C, P), 1)
        rowc = lax.broadcasted_iota(jnp.int32, (C, P), 0)
        lo = lane < N
        col_in = jnp.where(lo, lane, lane - N)
        incl2 = rowc >= col_in
        strict2 = rowc > col_in
        lo2 = lax.broadcasted_iota(jnp.int32, (2 * C, P), 1) < N
        bd_mask = (lax.broadcasted_iota(jnp.int32, (P, P), 0) < N) == (lax.broadcasted_iota(jnp.int32, (P, P), 1) < N)
        pairs = [slice(p * P, (p + 1) * P) for p in range(n_pairs)]

        def chunk_factors(c, r_all, lw_all, kd_all, v_all, kk_all, a_all, cum_all):
            rs = slice(c * C, (c + 1) * C)
            r, lw, kd, v, kk, a, cum = (t[rs] for t in (r_all, lw_all, kd_all, v_all, kk_all, a_all, cum_all))
            m = cum[C // 2 - 1:C // 2, :]
            cend = cum[C - 1:C, :]
            e_pos = jnp.exp(cum - m)
            e_neg = jnp.exp(m - cum)
            e_prev = jnp.exp(cum - lw - m)
            e_m = jnp.exp(m)
            e_end = jnp.exp(cend - m)
            rp = r * e_pos
            kkp = kk * e_prev
            kp = kd * e_neg
            bp = kk * a * e_neg
            rhat = rp * e_m
            kkhat = kkp * e_m
            ktil = kp * e_end
            btil = bp * e_end
            e_cend = e_end * e_m
            return dict(rp=rp, kkp=kkp, kp=kp, bp=bp, rhat=rhat, kkhat=kkhat, ktil=ktil, btil=btil, v=v,
                        e_cend=e_cend)

        def local_units(facs, between_stages):
            units = [(f, ps) for f in facs for ps in pairs]
            scores = []
            for f, ps in units:
                q_pair = jnp.concatenate([f["rp"][:, ps], f["kkp"][:, ps]], axis=0)
                q1 = jnp.where(lo2, q_pair, 0.0).astype(BF16)
                q2 = jnp.where(lo2, 0.0, q_pair).astype(BF16)
                kp_b, bp_b = f["kp"][:, ps].astype(BF16), f["bp"][:, ps].astype(BF16)
                scores.append((_dot_nt(q1, jnp.concatenate([bp_b, kp_b], axis=0)),
                               _dot_nt(q2, jnp.concatenate([kp_b, bp_b], axis=0))))
            between_stages()
            ar1, ar2, n_bd, avs, v_b = [], [], [], [], []
            for (f, ps), (sc1, sc2) in zip(units, scores):
                ar1.append(jnp.where(incl2, sc1[:C], 0.0).astype(BF16))
                ar2.append(jnp.where(incl2, sc2[:C], 0.0).astype(BF16))
                kn1 = jnp.where(strict2, sc1[C:], 0.0)
                kn2 = jnp.where(strict2, sc2[C:], 0.0)
                n_bd.append(jnp.concatenate([jnp.where(lo, kn1, 0.0), jnp.where(lo, 0.0, kn2)], axis=0).astype(BF16))
                akk = jnp.concatenate([jnp.where(lo, 0.0, kn1), jnp.where(lo, kn2, 0.0)], axis=0).astype(BF16)
                v_pair = f["v"][:, ps]
                v_b.append(v_pair.astype(BF16))
                v_sw = pltpu.roll(v_pair, N, 1).astype(BF16)
                avs.append(_dot(akk, jnp.concatenate([v_sw, v_sw], axis=0)))
            between_stages()
            xs = []
            for (f, ps), av in zip(units, avs):
                kkh = f["kkhat"][:, ps]
                xs.append(jnp.concatenate([jnp.where(lo, kkh, av[:C]), jnp.where(lo, av[C:], kkh)], axis=0))
            pw = n_bd
            span = 1
            while span < C:
                last = 2 * span >= C
                rhs = [x.astype(BF16) if last else jnp.concatenate([t, x.astype(BF16)], axis=1)
                       for t, x in zip(pw, xs)]
                prod = [_dot(t, r) for t, r in zip(pw, rhs)]
                px = prod if last else [r[:, P:] for r in prod]
                xs = [x - d for x, d in zip(xs, px)] if span == 1 else [x + d for x, d in zip(xs, px)]
                if not last:
                    pw = [r[:, :P].astype(BF16) for r in prod]
                between_stages()
                span *= 2
            zeros_cp = jnp.zeros((C, P), BF16)
            zs, w_parts = [], []
            for (f, ps), x, a1, a2, vb in zip(units, xs, ar1, ar2, v_b):
                w1 = jnp.where(lo, x[:C], x[C:])
                w2 = pltpu.roll(jnp.where(lo, x[C:], x[:C]), N, 1)
                w1_hi, w1_lo = _split_bf16(w1)
                w2_hi, w2_lo = _split_bf16(w2)
                nw = jnp.concatenate([-w1_hi, -w2_hi], axis=1).astype(BF16)
                zv = jnp.concatenate([zeros_cp, vb], axis=1)
                zs.append((_dot(a1, jnp.concatenate([nw, zv], axis=0)),
                           _dot(a2, jnp.concatenate([zv, nw], axis=0))))
                w_parts.append((w1_hi, w1_lo, w2_hi, w2_lo))
            out = []
            for (f, ps), (z1, z2), (w1_hi, w1_lo, w2_hi, w2_lo), vb in zip(units, zs, w_parts, v_b):
                qeff = (f["rhat"][:, ps] + jnp.where(lo, z1[:, :P], z2[:, :P])).astype(BF16)
                yloc = jnp.where(lo, z1[:, P:], z2[:, P:])
                kt_b, bt_b = f["ktil"][:, ps].astype(BF16), f["btil"][:, ps].astype(BF16)
                w1tb = _dot_tn(jnp.concatenate([w1_hi, w1_lo], axis=0).astype(BF16),
                               jnp.concatenate([bt_b, bt_b], axis=0))
                hm = _dot_tn(jnp.concatenate([vb, (-w2_hi).astype(BF16), (-w2_lo).astype(BF16)], axis=0),
                             jnp.concatenate([kt_b, bt_b, bt_b], axis=0))
                out.append(dict(qeff=qeff, yloc=yloc, g_bd=jnp.where(bd_mask, -w1tb, 0.0).astype(BF16),
                                h_bd=jnp.where(bd_mask, hm, 0.0), e_cend=f["e_cend"][:, ps]))
            return [out[c * n_pairs:(c + 1) * n_pairs] for c in range(len(facs))]

        def state_part(loc, states):
            ys, new_states = [], []
            for u, s_bd in zip(loc, states):
                s_hi, s_lo = _split_bf16(s_bd)
                s_hi_b = s_hi.astype(BF16)
                ys.append(_dot_nt(u["qeff"], s_hi_b) + u["yloc"])
                g2 = jnp.concatenate([u["g_bd"], u["g_bd"]], axis=0)
                sg = _dot(jnp.concatenate([s_hi_b, s_lo.astype(BF16)], axis=1), g2)
                new_states.append(s_bd * u["e_cend"] + sg + u["h_bd"])
            return jnp.concatenate(ys, axis=1), new_states

        @pl.when(i == 0)
        def _():
            for _ in prep_pieces(zb0_ref[...], jnp.zeros_like(prev_scr)):
                pass

        last_row = prev_scr[...]
        facs = []
        for c in range(R // C):
            f = {name: fac_scr[j, c * C:(c + 1) * C, :] for j, name in enumerate(_FAC_NAMES)}
            f["e_cend"] = ec_scr[c:c + 1, :]
            facs.append(f)
        g, bonus = gb_scr[0], gb_scr[1]

        next_starts_seq = lax.rem(i + 1, steps_per_seq) == 0
        next_prep = prep_pieces(zbn_ref[...], jnp.where(next_starts_seq, 0.0, last_row))
        locs = local_units(facs, lambda: next(next_prep, None))
        states = [state_scr[p] for p in range(n_pairs)]
        y_rows = []
        for loc in locs:
            y_c, states = state_part(loc, states)
            y_rows.append(y_c)
        for p in range(n_pairs):
            state_scr[p] = states[p]

        y = jnp.concatenate(y_rows, axis=0)
        y_ref[...] = _rwkv_finish(y, bonus, g, vec, ones_bd).astype(BF16)
        for _ in next_prep:
            pass

        @pl.when(s_idx == steps_per_seq - 1)
        def _():
            shift_ref[0] = last_row
            for p in range(n_pairs):
                sout_ref[0, 2 * p] = states[p][:N, :N]
                sout_ref[0, 2 * p + 1] = states[p][N:, N:]


def _rwkv_prompt(zb, y_sample, params, layer, batch, seq):
    mu, vec, wup, aup, gup, ones_bd = params
    pb = zb.shape[1]
    wb = vec.shape[2]
    n_heads = wb // B_HEAD
    R = RWKV_STEP_ROWS
    sps = seq // R
    n_steps = batch * sps
    n_s = y_sample.shape[0]
    m = batch * seq + n_s
    seq_of = lambda i: jnp.minimum(i // sps, batch - 1)
    return pl.pallas_call(
        functools.partial(_rwkv_prompt_kernel, steps_per_seq=sps, n_prompt_steps=n_steps),
        grid=(n_steps + n_s // R,),
        in_specs=[
            pl.BlockSpec((R, pb), lambda i: (0, 0), pipeline_mode=pl.Buffered(1)),
            pl.BlockSpec((R, pb), lambda i: (jnp.minimum(i + 1, n_steps - 1), 0)),
            pl.BlockSpec((R, wb), lambda i: (jnp.maximum(i - n_steps, 0), 0)),
            _layer_block(mu, layer), _layer_block(vec, layer), _layer_block(wup, layer),
            _layer_block(aup, layer), _layer_block(gup, layer), _whole(ones_bd),
        ],
        out_specs=[
            pl.BlockSpec((R, wb), lambda i: (i, 0)),
            pl.BlockSpec((1, n_heads, B_HEAD, B_HEAD), lambda i: (seq_of(i), 0, 0, 0)),
            pl.BlockSpec((1, 1, pb), lambda i: (seq_of(i), 0, 0)),
        ],
        out_shape=[
            jax.ShapeDtypeStruct((m, wb), BF16),
            jax.ShapeDtypeStruct((batch, n_heads, B_HEAD, B_HEAD), F32),
            jax.ShapeDtypeStruct((batch, 1, pb), F32),
        ],
        scratch_shapes=[pltpu.VMEM((n_heads // 2, 2 * B_HEAD, 2 * B_HEAD), F32), pltpu.VMEM((1, pb), F32),
                        pltpu.VMEM((len(_FAC_NAMES), R, wb), F32), pltpu.VMEM((2, R, wb), F32),
                        pltpu.VMEM((SUBLANES, wb), F32)],
        compiler_params=_cparams("arbitrary"),
        name="rwkv_prompt",
    )(zb, zb, y_sample, mu, vec, wup, aup, gup, ones_bd)


def _rwkv_sample_kernel(zb_ref, shift_ref, mu_ref, vec_ref, wup_ref, aup_ref, gup_ref, ones_ref, s_ref, buf_ref,
                        y_ref, sout_ref, shift_out_ref,
                        kk_t, wr_t, b_t, kd_t, v_t, w_t, r_t, y_t, g_scr, bonus_scr):
    del buf_ref
    p = pl.program_id(1)
    heads_per_step = s_ref.shape[0]
    N = B_HEAD

    @pl.when(p == 0)
    def _():
        zb = zb_ref[...]
        shift_out_ref[...] = zb
        r, lw, kd, v, kk, a, g, bonus = _rwkv_prep(
            zb, shift_ref[...], mu_ref[...], vec_ref[...], wup_ref[...], aup_ref[...], gup_ref[...], ones_ref[...])
        w = jnp.exp(lw)
        kk_t[...] = kk.T
        wr_t[...] = (w * r).T
        b_t[...] = (kk * a).T
        kd_t[...] = kd.T
        v_t[...] = v.T
        w_t[...] = w.T
        r_t[...] = r.T
        g_scr[...] = g
        bonus_scr[...] = bonus

    for hh in range(heads_per_step):
        base = pl.multiple_of((p * heads_per_step + hh) * N, N)
        hs = pl.ds(base, N)
        kk_h, wr_h, b_h, kd_h, w_h, r_h = kk_t[hs, :], wr_t[hs, :], b_t[hs, :], kd_t[hs, :], w_t[hs, :], r_t[hs, :]
        b_dot_r = jnp.sum(b_h * r_h, axis=0, keepdims=True)
        k_dot_r = jnp.sum(kd_h * r_h, axis=0, keepdims=True)

        def body(vi, carry):
            s0 = s_ref[hh, vi]
            s_kk = jnp.sum(s0 * kk_h, axis=0, keepdims=True)
            yq = jnp.sum(s0 * wr_h, axis=0, keepdims=True)
            v_row = v_t[pl.ds(base + vi, 1), :]
            sout_ref[hh, vi] = s0 * w_h - s_kk * b_h + v_row * kd_h
            y_t[pl.ds(base + vi, 1), :] = yq - s_kk * b_dot_r + v_row * k_dot_r
            return carry

        lax.fori_loop(0, N, body, 0, unroll=4)

    @pl.when(p == pl.num_programs(1) - 1)
    def _():
        y = y_t[...].T
        y_ref[...] = _rwkv_finish(y, bonus_scr[...], g_scr[...], vec_ref[...], ones_ref[...]).astype(BF16)


def _rwkv_sample(zb, shift, state_t, out_buf, params, layer, n_prompt):
    mu, vec, wup, aup, gup, ones_bd = params
    pb = zb.shape[1]
    wb = vec.shape[2]
    n_heads, n_s = state_t.shape[1], state_t.shape[4]
    hps = 2
    off = n_prompt // LANES
    sspec = pl.BlockSpec((None, hps, B_HEAD, B_HEAD, LANES), lambda sb, p: (layer, p, 0, 0, sb))
    return pl.pallas_call(
        _rwkv_sample_kernel,
        grid=(n_s // LANES, n_heads // hps),
        in_specs=[
            pl.BlockSpec((LANES, pb), lambda sb, p: (off + sb, 0)),
            pl.BlockSpec((None, LANES, pb), lambda sb, p: (layer, sb, 0)),
            _layer_block(mu, layer), _layer_block(vec, layer), _layer_block(wup, layer),
            _layer_block(aup, layer), _layer_block(gup, layer), _whole(ones_bd),
            sspec, pl.BlockSpec(memory_space=pl.ANY),
        ],
        out_specs=[pl.BlockSpec((LANES, wb), lambda sb, p: (sb, 0)), sspec,
                   pl.BlockSpec((LANES, pb), lambda sb, p: (sb, 0))],
        out_shape=[jax.ShapeDtypeStruct((n_s, wb), BF16), jax.ShapeDtypeStruct(state_t.shape, F32),
                   jax.ShapeDtypeStruct((n_s, pb), F32)],
        input_output_aliases={9: 1},
        scratch_shapes=[pltpu.VMEM((wb, LANES), F32)] * 8 + [pltpu.VMEM((LANES, wb), F32)] * 2,
        compiler_params=_cparams("arbitrary", "arbitrary"),
        name="rwkv_sample",
    )(zb, shift, mu, vec, wup, aup, gup, ones_bd, state_t, out_buf)


def _hgrn_gates(fz, clb, layer):
    tail = jnp.log1p(jnp.exp(-jnp.abs(fz)))
    ls_pos = jnp.minimum(fz, 0.0) - tail
    ls_neg = jnp.minimum(-fz, 0.0) - tail
    sig_neg = jnp.exp(ls_neg)
    if layer == 0:
        return ls_pos, sig_neg
    e = jnp.exp(clb - jnp.max(clb, axis=0, keepdims=True))
    sm = e / jnp.sum(e, axis=0, keepdims=True)
    lb = jnp.sum(sm[1:layer + 1], axis=0, keepdims=True)
    x2 = jnp.log(lb) + ls_neg
    log_f = jnp.maximum(ls_pos, x2) + jnp.log1p(jnp.exp(-jnp.abs(ls_pos - x2)))
    return log_f, (1.0 - lb) * sig_neg


def _hgrn_finish(o, g, norm_g):
    on = o * lax.rsqrt(jnp.mean(o * o, axis=-1, keepdims=True) + RMS_EPS) * norm_g
    return on * (g * _sigmoid(g))


def _hgrn_prompt_kernel(zc_ref, ys_ref, clb_ref, ng_ref, y_ref, sout_ref, state_scr, *, layer, n_chunks, n_prompt_steps):
    i = pl.program_id(0)

    @pl.when(i >= n_prompt_steps)
    def _():
        y_ref[...] = ys_ref[...]

    @pl.when(i < n_prompt_steps)
    def _():
        c_idx = lax.rem(i, n_chunks)

        @pl.when(c_idx == 0)
        def _():
            state_scr[...] = jnp.zeros_like(state_scr)

        z = zc_ref[...]
        C = z.shape[0]
        wc = z.shape[1] // 4
        n_heads = wc // C_HEAD
        D = C_HEAD
        SUB = HGRN_SUB
        HALF = min(SUBLANES, SUB)
        nsub = C // SUB
        q, fz, iv, g = z[:, :wc], z[:, wc:2 * wc], z[:, 2 * wc:3 * wc], z[:, 3 * wc:]
        log_f, kg = _hgrn_gates(fz, clb_ref[...], layer)
        bcum = _cumsum_rows(log_f)
        e_b = jnp.exp(bcum)
        b_end = bcum[C - 1:C, :]
        e_end = jnp.exp(b_end)
        k_hat = kg * jnp.exp(b_end - bcum)
        t_full = lax.broadcasted_iota(jnp.int32, (nsub, SUB, 1), 1)
        t_half = lax.broadcasted_iota(jnp.int32, (nsub, max(SUB - HALF, 1), 1), 1) + HALF
        level_sizes = [SUB << li for li in range((C // SUB).bit_length() - 1)]
        half = C // 2
        n_lv = len(level_sizes) * half
        ri = lax.broadcasted_iota(jnp.int32, (n_lv, n_lv), 0)
        ci = lax.broadcasted_iota(jnp.int32, (n_lv, n_lv), 1)
        level_mask = None
        for li, s in enumerate(level_sizes):
            shift = s.bit_length() - 1
            in_level = (ri >= li * half) & (ri < (li + 1) * half) & (ci >= li * half) & (ci < (li + 1) * half)
            same_block = ((ri - li * half) >> shift) == ((ci - li * half) >> shift)
            lm = in_level & same_block
            level_mask = lm if level_mask is None else (level_mask | lm)

        outs = []
        for h in range(n_heads):
            sl = slice(h * D, (h + 1) * D)
            qh, kh, vh, bh = q[:, sl], kg[:, sl], iv[:, sl], bcum[:, sl]
            st = state_scr[h]
            o = _bdot_nt(qh * e_b[:, sl], st)
            q3, k3, v3, b3 = (t.reshape(nsub, SUB, D) for t in (qh, kh, vh, bh))
            od = jnp.zeros((nsub, SUB, D), F32)
            if SUB > HALF:
                q3h, b3h = q3[:, HALF:, :], b3[:, HALF:, :]
                odh = jnp.zeros((nsub, SUB - HALF, D), F32)
            for j in range(SUB):
                if j < HALF:
                    dec = jnp.exp(b3 - b3[:, j:j + 1, :])
                    att = jnp.sum(q3 * k3[:, j:j + 1, :] * dec, axis=-1, keepdims=True)
                    od = od + jnp.where(t_full >= j, att, 0.0) * v3[:, j:j + 1, :]
                else:
                    dec = jnp.exp(b3h - b3[:, j:j + 1, :])
                    att = jnp.sum(q3h * k3[:, j:j + 1, :] * dec, axis=-1, keepdims=True)
                    odh = odh + jnp.where(t_half >= j, att, 0.0) * v3[:, j:j + 1, :]
            if SUB > HALF:
                od = od + jnp.concatenate([jnp.zeros((nsub, HALF, D), F32), odh], axis=1)
            o = o + od.reshape(C, D)
            q_l, k_l, v_l = [], [], []
            for s in level_sizes:
                for j in range(C // (2 * s)):
                    a0 = 2 * j * s
                    b_bound = bh[a0 + s - 1:a0 + s, :]
                    q_l.append(qh[a0 + s:a0 + 2 * s, :] * jnp.exp(bh[a0 + s:a0 + 2 * s, :] - b_bound))
                    k_l.append(kh[a0:a0 + s, :] * jnp.exp(b_bound - bh[a0:a0 + s, :]))
                    v_l.append(vh[a0:a0 + s, :])
            att = _bdot_nt(jnp.concatenate(q_l, axis=0), jnp.concatenate(k_l, axis=0))
            o_lv = _bdot(jnp.where(level_mask, att, 0.0), jnp.concatenate(v_l, axis=0))
            for li, s in enumerate(level_sizes):
                pieces = []
                for j in range(C // (2 * s)):
                    r0 = li * (C // 2) + j * s
                    pieces += [jnp.zeros((s, D), F32), o_lv[r0:r0 + s, :]]
                o = o + jnp.concatenate(pieces, axis=0)
            outs.append(o)
            vh_hi, vh_lo = _split_bf16(vh)
            kh_hi, kh_lo = _split_bf16(k_hat[:, sl])
            upd = _dot_tn(jnp.concatenate([vh_hi, vh_hi, vh_lo], axis=0).astype(BF16),
                          jnp.concatenate([kh_hi, kh_lo, kh_hi], axis=0).astype(BF16))
            state_scr[h] = st * e_end[:, sl] + upd

        o_all = jnp.concatenate(outs, axis=1)
        y_ref[...] = _hgrn_finish(o_all, g, ng_ref[...]).astype(BF16)

        @pl.when(c_idx == n_chunks - 1)
        def _():
            for h in range(n_heads):
                sout_ref[0, h] = state_scr[h].T


def _hgrn_prompt(zc, y_sample, clb, norm_g, layer, batch, seq):
    pc = zc.shape[1]
    wc = pc // 4
    n_heads = wc // C_HEAD
    C = C_CHUNK
    nc = seq // C
    n_steps = batch * nc
    n_s = y_sample.shape[0]
    m = batch * seq + n_s
    return pl.pallas_call(
        functools.partial(_hgrn_prompt_kernel, layer=layer, n_chunks=nc, n_prompt_steps=n_steps),
        grid=(n_steps + n_s // C,),
        in_specs=[
            pl.BlockSpec((C, pc), lambda i: (jnp.minimum(i, n_steps - 1), 0)),
            pl.BlockSpec((C, wc), lambda i: (jnp.maximum(i - n_steps, 0), 0)),
            _whole(clb), _layer_block(norm_g, layer),
        ],
        out_specs=[
            pl.BlockSpec((C, wc), lambda i: (i, 0)),
            pl.BlockSpec((1, n_heads, C_HEAD, C_HEAD), lambda i: (jnp.minimum(i // nc, batch - 1), 0, 0, 0)),
        ],
        out_shape=[
            jax.ShapeDtypeStruct((m, wc), BF16),
            jax.ShapeDtypeStruct((batch, n_heads, C_HEAD, C_HEAD), F32),
        ],
        scratch_shapes=[pltpu.VMEM((n_heads, C_HEAD, C_HEAD), F32)],
        compiler_params=_cparams("arbitrary"),
        name="hgrn_prompt",
    )(zc, y_sample, clb, norm_g)


def _hgrn_sample_kernel(zc_ref, clb_ref, ng_ref, s_ref, buf_ref, y_ref, sout_ref,
                        q_scr, f_scr, k_scr, v_scr, o_scr, *, layer):
    del buf_ref
    z = zc_ref[...]
    n_rows, n_heads = s_ref.shape[0], s_ref.shape[1]
    wc = z.shape[1] // 4
    D = C_HEAD
    q, fz, iv, g = z[:, :wc], z[:, wc:2 * wc], z[:, 2 * wc:3 * wc], z[:, 3 * wc:]
    log_f, kg = _hgrn_gates(fz, clb_ref[...], layer)
    q_scr[...] = q
    f_scr[...] = jnp.exp(log_f)
    k_scr[...] = kg
    v_scr[...] = iv
    pad = jnp.zeros((SUBLANES - n_heads, D), F32)

    def body(i, carry):
        q_r, f_r, k_r, v_r = (t[pl.ds(i, 1), :] for t in (q_scr, f_scr, k_scr, v_scr))
        heads = lambda t: jnp.concatenate([t[:, h * D:(h + 1) * D] for h in range(n_heads)] + [pad], axis=0)
        f_cols = heads(f_r).T
        k_cols = heads(k_r).T
        os_ = []
        for h in range(n_heads):
            sl = slice(h * D, (h + 1) * D)
            s0 = s_ref[i, h]
            qk = jnp.sum(q_r[:, sl] * k_r[:, sl], axis=-1, keepdims=True)
            os_.append(_bdot(q_r[:, sl] * f_r[:, sl], s0) + qk * v_r[:, sl])
            sout_ref[i, h] = s0 * f_cols[:, h:h + 1] + k_cols[:, h:h + 1] * v_r[:, sl]
        o_scr[pl.ds(i, 1), :] = jnp.concatenate(os_, axis=1)
        return carry

    lax.fori_loop(0, n_rows, body, 0, unroll=4)
    y_ref[...] = _hgrn_finish(o_scr[...], g, ng_ref[...]).astype(BF16)


def _hgrn_sample(zc, clb, norm_g, state, out_buf, layer, n_prompt):
    pc = zc.shape[1]
    wc = pc // 4
    n_s, n_heads = state.shape[1], state.shape[2]
    R = SAMPLE_ROWS
    off = n_prompt // R
    sspec = pl.BlockSpec((None, R, n_heads, C_HEAD, C_HEAD), lambda i: (layer, i, 0, 0, 0))
    return pl.pallas_call(
        functools.partial(_hgrn_sample_kernel, layer=layer),
        grid=(n_s // R,),
        in_specs=[pl.BlockSpec((R, pc), lambda i: (off + i, 0)), _whole(clb), _layer_block(norm_g, layer),
                  sspec, pl.BlockSpec(memory_space=pl.ANY)],
        out_specs=[pl.BlockSpec((R, wc), lambda i: (i, 0)), sspec],
        out_shape=[jax.ShapeDtypeStruct((n_s, wc), BF16), jax.ShapeDtypeStruct(state.shape, F32)],
        input_output_aliases={4: 1},
        scratch_shapes=[pltpu.VMEM((R, wc), F32)] * 5,
        compiler_params=_cparams("parallel"),
        name="hgrn_sample",
    )(zc, clb, norm_g, state, out_buf)


def _out_kernel(ya_ref, yb_ref, yc_ref, h_ref, wo_ref, ln_ref, p_ref, wpg_ref, wpp_ref, hb_ref, res_ref, *, alpha):
    wa, wb = ya_ref.shape[1], yb_ref.shape[1]
    mix = (_dot(ya_ref[...], wo_ref[:wa, :]) + _dot(yb_ref[...], wo_ref[wa:wa + wb, :])
           + _dot(yc_ref[...], wo_ref[wa + wb:, :]))
    h1 = _layer_norm(alpha * h_ref[...] + mix, ln_ref[0:1, :], ln_ref[1:2, :], LN_EPS)
    h1b = h1.astype(BF16)
    ple = _sigmoid(_dot(h1b, wpg_ref[...])) * _dot(p_ref[...], wpp_ref[...])
    hb_ref[...] = h1b
    res_ref[...] = alpha * h1 + ple


def _out_proj(ya, yb, yc, h, wo, ln, p, wpg, wpp, layer, alpha):
    m, d = h.shape
    tm = _pick_tile(m, 416, 16)
    rows = lambda a: pl.BlockSpec((tm, a.shape[-1]), lambda i: (i, 0))
    const = lambda a: pl.BlockSpec((None,) + a.shape[1:], lambda i: (layer, 0, 0), pipeline_mode=pl.Buffered(1))
    return pl.pallas_call(
        functools.partial(_out_kernel, alpha=alpha),
        grid=(m // tm,),
        in_specs=[rows(ya), rows(yb), rows(yc), rows(h), const(wo), const(ln),
                  pl.BlockSpec((None, tm, p.shape[-1]), lambda i: (layer, i, 0)), const(wpg), const(wpp)],
        out_specs=[rows(h), rows(h)],
        out_shape=[jax.ShapeDtypeStruct((m, d), BF16), jax.ShapeDtypeStruct((m, d), F32)],
        compiler_params=_cparams("parallel"),
        name="out_proj",
    )(ya, yb, yc, h, wo, ln, p, wpg, wpp)


def _ffn_kernel(hb_ref, res_ref, wg_ref, wu_ref, wd_ref, ln_ref, h_ref, hbo_ref, acc_ref):
    f = pl.program_id(1)

    @pl.when(f == 0)
    def _():
        acc_ref[...] = jnp.zeros_like(acc_ref)

    x = hb_ref[...]
    gate = _dot(x, wg_ref[...])
    up = _dot(x, wu_ref[...])
    act = (gate * _sigmoid(gate) * up).astype(BF16)
    acc_ref[...] += _dot(act, wd_ref[...])

    @pl.when(f == pl.num_programs(1) - 1)
    def _():
        h2 = _layer_norm(res_ref[...] + acc_ref[...], ln_ref[0:1, :], ln_ref[1:2, :], LN_EPS)
        h_ref[...] = h2
        hbo_ref[...] = h2.astype(BF16)


def _ffn(hb, res, wg, wu, wd, ln, layer):
    m, d = hb.shape
    dff = wg.shape[2]
    tm = _pick_tile(m, 640, 16)
    tf = _pick_tile(dff, 512, 128)
    return pl.pallas_call(
        _ffn_kernel,
        grid=(m // tm, dff // tf),
        in_specs=[
            pl.BlockSpec((tm, d), lambda i, f: (i, 0)),
            pl.BlockSpec((tm, d), lambda i, f: (i, 0)),
            pl.BlockSpec((None, d, tf), lambda i, f: (layer, 0, f)),
            pl.BlockSpec((None, d, tf), lambda i, f: (layer, 0, f)),
            pl.BlockSpec((None, tf, d), lambda i, f: (layer, f, 0)),
            pl.BlockSpec((None, 2, d), lambda i, f: (layer, 0, 0)),
        ],
        out_specs=[pl.BlockSpec((tm, d), lambda i, f: (i, 0)), pl.BlockSpec((tm, d), lambda i, f: (i, 0))],
        out_shape=[jax.ShapeDtypeStruct((m, d), F32), jax.ShapeDtypeStruct((m, d), BF16)],
        scratch_shapes=[pltpu.VMEM((tm, d), F32)],
        compiler_params=_cparams("parallel", "arbitrary"),
        name="ffn",
    )(hb, res, wg, wu, wd, ln)


def kernel(x_prompt, x_sample, state_rwkv, state_shift, state_hgrn, p_prompt, p_sample, ln_in_g, ln_in_b, w_in, a_ln_g, a_ln_b, a_ws, a_bs, b_mu, b_w0, b_w_up, b_a0, b_a_up, b_g_up, b_k_k, b_k_a, b_r_k, b_gn_g, b_gn_b, c_lower_bounds, c_norm_g, w_out, ln1_g, ln1_b, w_ffn_gate, w_ffn_up, w_ffn_down, w_ple_gate, w_ple_proj, ln2_g, ln2_b):
    batch, seq, d = x_prompt.shape
    n_s = x_sample.shape[0]
    depth = w_in.shape[0]
    n_p = batch * seq
    wa, wb, wc = a_ln_g.shape[1], b_w0.shape[1], c_norm_g.shape[1]
    pa, pb = 2 * wa, 3 * wb + LORA_W + LORA_A + LORA_G
    alpha = float((2 * depth) ** 0.25)
    assert x_sample.shape[1] == 1 and seq % A_CHUNK == 0 and seq % C_CHUNK == 0 and seq % RWKV_STEP_ROWS == 0
    assert n_s % ROW_BLOCK == 0 and n_p % ROW_BLOCK == 0 and n_s % LANES == 0
    assert (wb // B_HEAD) % 2 == 0 and wb % MXU_TILE == 0

    bf = lambda t: t.astype(BF16)
    pc = w_in.shape[2] - pa - pb
    w_out_b, w_gate_b, w_up_b, w_down_b = bf(w_out), bf(w_ffn_gate), bf(w_ffn_up), bf(w_ffn_down)
    w_pg_b, w_pp_b = bf(w_ple_gate), bf(w_ple_proj)
    p_all = bf(jnp.concatenate([p_prompt.reshape(depth, n_p, -1), p_sample.reshape(depth, n_s, -1)], axis=1))
    sgu_ln = jnp.stack([a_ln_g, a_ln_b], axis=1)
    sgu_bst = jnp.swapaxes(a_bs, 1, 2)
    rwkv_vec = jnp.stack([b_w0, b_a0, b_k_k, b_k_a, b_r_k.reshape(depth, wb), b_gn_g, b_gn_b, jnp.zeros_like(b_w0)], axis=1)
    idx = jnp.arange(MXU_TILE)
    ones_bd = bf(idx[:, None] // B_HEAD == idx[None, :] // B_HEAD)
    rwkv_params = (b_mu.reshape(depth, 1, pb), rwkv_vec, bf(b_w_up), bf(b_a_up), bf(b_g_up), ones_bd)
    hgrn_ng = c_norm_g.reshape(depth, 1, wc)
    ln1 = jnp.stack([ln1_g, ln1_b], axis=1)
    ln2 = jnp.stack([ln2_g, ln2_b], axis=1)
    row = lambda t: t.reshape(1, -1)

    h, hb = _ln_in(x_prompt.reshape(n_p, d), x_sample.reshape(n_s, d), row(ln_in_g), row(ln_in_b))

    state_rwkv_t = jnp.transpose(state_rwkv, (0, 2, 3, 4, 1))
    rwkv_s = jnp.zeros(state_rwkv_t.shape, F32)
    hgrn_s = jnp.zeros(state_hgrn.shape, F32)
    rwkv_p, shift_p, hgrn_p, shift_s, sgu_v = [], [], [], [], []
    for l in range(depth):
        za = _mm(hb, w_in, l, 0, pa, "proj_a")
        zb = _mm(hb, w_in, l, pa, pb, "proj_b")
        zc = _mm(hb, w_in, l, pa + pb, pc, "proj_c")

        ya, v_rows = _sgu(za, sgu_ln, a_ws, sgu_bst, l, n_p)
        yb_s, rwkv_s, sh_s = _rwkv_sample(zb, state_shift, state_rwkv_t, rwkv_s, rwkv_params, l, n_p)
        yb, r_p, sh_p = _rwkv_prompt(zb, yb_s, rwkv_params, l, batch, seq)
        yc_s, hgrn_s = _hgrn_sample(zc, c_lower_bounds, hgrn_ng, state_hgrn, hgrn_s, l, n_p)
        yc, c_p = _hgrn_prompt(zc, yc_s, c_lower_bounds, hgrn_ng, l, batch, seq)

        hb, res = _out_proj(ya, yb, yc, h, w_out_b, ln1, p_all, w_pg_b, w_pp_b, l, alpha)
        h, hb = _ffn(hb, res, w_gate_b, w_up_b, w_down_b, ln2, l)

        rwkv_p.append(r_p)
        shift_p.append(sh_p.reshape(batch, pb))
        hgrn_p.append(c_p)
        shift_s.append(sh_s)
        sgu_v.append(v_rows.reshape(n_s, 1, wa))

    return (h[:n_p].reshape(batch, seq, d), h[n_p:].reshape(n_s, 1, d), jnp.stack(rwkv_p), jnp.stack(shift_p),
            jnp.stack(hgrn_p), jnp.transpose(rwkv_s, (0, 4, 1, 2, 3)), jnp.stack(shift_s), hgrn_s, jnp.stack(sgu_v))
```
